```python
import jax, jax.numpy as jnp
from jax import lax
import numpy as np

D_MODEL = 1024
BATCH = 4
SEQ = 8192
DEPTH = 2

N_EVEN = (DEPTH + 1) // 2
N_ODD = DEPTH // 2
BLOCK = 128
NORM_EPS = 1e-6
ADA_CHUNKS = 6

A_HEADS = 8
A_KV_HEADS = 2
A_GROUP = A_HEADS // A_KV_HEADS
A_HEAD_DIM = 64
A_WINDOW = 128
A_Q = A_HEADS * A_HEAD_DIM
A_KV = A_KV_HEADS * A_HEAD_DIM

B_GROUPS = 8
B_GROUP_DIM = 64
B_WIDTH = B_GROUPS * B_GROUP_DIM
B_CHUNK = 128

C_GROUPS = 8
C_GROUP_DIM = 64
C_WIDTH = C_GROUPS * C_GROUP_DIM
C_CONV = 3

D_HEADS = 8
D_NOPE = 64
D_ROPE = 32
D_VDIM = 64
D_Q_RANK = 512
D_KV_RANK = 256
ROPE_THETA = 10000.0

EVEN_IN = A_Q + 2 * A_KV + 2 * B_WIDTH
EVEN_MIX = A_Q + B_WIDTH
ODD_IN = 3 * C_WIDTH + D_Q_RANK + D_KV_RANK + D_ROPE
ODD_MIX = C_WIDTH + D_HEADS * D_VDIM

FF_DENSE = 2816
N_EXPERTS = 8
TOP_K = 2
FF_EXPERT = 3584
MOE_BLOCK = 512

kernel_name = 'hybrid_swa_gmlp_conv_mla_moe_block'


def rmsnorm(x, g):
    xf = x.astype(jnp.float32)
    r = lax.rsqrt(jnp.mean(xf * xf, axis=-1, keepdims=True) + NORM_EPS)
    return (xf * r).astype(x.dtype) * g


def modulate(x, g, shift, scale):
    return rmsnorm(x, g) * (1.0 + scale[:, None, :]) + shift[:, None, :]


def alibi_slopes(n):
    return 2.0 ** (-8.0 * jnp.arange(1, n + 1, dtype=jnp.float32) / n)


def swa_sink_attention(q, k, v, sinks):
    Bn, S = q.shape[0], q.shape[1]
    nb = S // BLOCK
    qb = q.reshape(Bn, nb, BLOCK, A_KV_HEADS, A_GROUP, A_HEAD_DIM)
    kb = k.reshape(Bn, nb, BLOCK, A_KV_HEADS, A_HEAD_DIM)
    vb = v.reshape(Bn, nb, BLOCK, A_KV_HEADS, A_HEAD_DIM)
    kband = jnp.concatenate([jnp.concatenate([jnp.zeros_like(kb[:, :1]), kb[:, :-1]], axis=1), kb], axis=2)
    vband = jnp.concatenate([jnp.concatenate([jnp.zeros_like(vb[:, :1]), vb[:, :-1]], axis=1), vb], axis=2)
    scores = jnp.einsum('bnqhgd,bnshd->bnhgqs', qb, kband).astype(jnp.float32) * (A_HEAD_DIM ** -0.5)
    qi = jnp.arange(BLOCK)[:, None]
    kj = jnp.arange(2 * BLOCK)[None, :]
    dist = qi + BLOCK - kj
    valid = (dist >= 0) & (dist < A_WINDOW)
    valid = valid[None] & ((jnp.arange(nb)[:, None, None] > 0) | (kj[None] >= BLOCK))
    slopes = alibi_slopes(A_HEADS).reshape(A_KV_HEADS, A_GROUP)
    scores = scores - slopes[:, :, None, None] * dist.astype(jnp.float32)
    scores = jnp.where(valid[None, :, None, None], scores, -jnp.inf)
    sink = sinks.astype(jnp.float32).reshape(A_KV_HEADS, A_GROUP)[None, None, :, :, None, None]
    m = jnp.maximum(jnp.max(scores, axis=-1, keepdims=True), sink)
    p = jnp.exp(scores - m)
    p = p / (jnp.sum(p, axis=-1, keepdims=True) + jnp.exp(sink - m))
    out = jnp.einsum('bnhgqs,bnshd->bnqhgd', p.astype(v.dtype), vband)
    return out.reshape(Bn, S, A_Q)


def chunk_spatial_gate(u, v, ln_g, ln_b, w_s, b_s):
    Bn, S, _ = v.shape
    vf = v.astype(jnp.float32)
    mu = jnp.mean(vf, axis=-1, keepdims=True)
    var = jnp.mean((vf - mu) ** 2, axis=-1, keepdims=True)
    vn = ((vf - mu) * lax.rsqrt(var + NORM_EPS)).astype(v.dtype) * ln_g + ln_b
    nc = S // B_CHUNK
    vc = vn.reshape(Bn, nc, B_CHUNK, B_GROUPS, B_GROUP_DIM)
    causal = jnp.tril(jnp.ones((B_CHUNK, B_CHUNK), dtype=bool))
    w = jnp.where(causal[None], w_s, 0.0).astype(v.dtype)
    s = jnp.einsum('gts,bnsgc->bntgc', w, vc) + b_s.T[None, None, :, :, None]
    return u * s.reshape(Bn, S, B_WIDTH)


def short_conv_mixer(gate_b, gate_c, xin, conv_w):
    S = xin.shape[1]
    z = gate_c * xin
    zp = jnp.pad(z, ((0, 0), (C_CONV - 1, 0), (0, 0)))
    y = sum(conv_w[j] * zp[:, j:j + S] for j in range(C_CONV))
    return gate_b * y


def apply_rope(x, cos, sin):
    half = x.shape[-1] // 2
    x1, x2 = x[..., :half], x[..., half:]
    cos = cos.astype(x.dtype)
    sin = sin.astype(x.dtype)
    return jnp.concatenate([x1 * cos - x2 * sin, x1 * sin + x2 * cos], axis=-1)


def mla_attention(q_lat, kv_lat, k_rope, q_norm_g, w_q_b, kv_norm_g, w_kv_b):
    Bn, S, _ = q_lat.shape
    q = (rmsnorm(q_lat, q_norm_g) @ w_q_b).reshape(Bn, S, D_HEADS, D_NOPE + D_ROPE)
    kv = (rmsnorm(kv_lat, kv_norm_g) @ w_kv_b).reshape(Bn, S, D_HEADS, D_NOPE + D_VDIM)
    q_nope, q_pe = q[..., :D_NOPE], q[..., D_NOPE:]
    k_nope, v = kv[..., :D_NOPE], kv[..., D_NOPE:]
    inv = ROPE_THETA ** (-jnp.arange(0, D_ROPE, 2, dtype=jnp.float32) / D_ROPE)
    ang = jnp.arange(S, dtype=jnp.float32)[:, None] * inv[None, :]
    cos, sin = jnp.cos(ang), jnp.sin(ang)
    q_pe = apply_rope(q_pe, cos[:, None, :], sin[:, None, :])
    k_pe = apply_rope(k_rope, cos, sin)
    scale = (D_NOPE + D_ROPE) ** -0.5
    nb = S // BLOCK
    qn_b = q_nope.reshape(Bn, nb, BLOCK, D_HEADS, D_NOPE).transpose(1, 0, 2, 3, 4)
    qp_b = q_pe.reshape(Bn, nb, BLOCK, D_HEADS, D_ROPE).transpose(1, 0, 2, 3, 4)
    kpos = jnp.arange(S)

    def one_block(args):
        qn, qp, blk = args
        s = (jnp.einsum('bqhd,bshd->bhqs', qn, k_nope)
             + jnp.einsum('bqhr,bsr->bhqs', qp, k_pe)).astype(jnp.float32) * scale
        qpos = blk * BLOCK + jnp.arange(BLOCK)
        s = jnp.where(kpos[None, :] <= qpos[:, None], s, -jnp.inf)
        p = jax.nn.softmax(s, axis=-1).astype(v.dtype)
        return jnp.einsum('bhqs,bshd->bqhd', p, v)

    out = lax.map(one_block, (qn_b, qp_b, jnp.arange(nb)))
    return out.transpose(1, 0, 2, 3, 4).reshape(Bn, S, D_HEADS * D_VDIM)


def even_mixer(h, w_in, sinks, ln_g, ln_b, w_s, b_s, w_o):
    Bn, S, _ = h.shape
    proj = h @ w_in
    q, k, v, u, g = jnp.split(proj, [A_Q, A_Q + A_KV, A_Q + 2 * A_KV, A_Q + 2 * A_KV + B_WIDTH], axis=-1)
    a_out = swa_sink_attention(q.reshape(Bn, S, A_HEADS, A_HEAD_DIM),
                               k.reshape(Bn, S, A_KV_HEADS, A_HEAD_DIM),
                               v.reshape(Bn, S, A_KV_HEADS, A_HEAD_DIM), sinks)
    b_out = chunk_spatial_gate(jax.nn.gelu(u, approximate=False), jax.nn.gelu(g, approximate=False),
                               ln_g, ln_b, w_s, b_s)
    return jnp.concatenate([a_out, b_out], axis=-1) @ w_o


def odd_mixer(h, w_in, conv_w, q_norm_g, w_q_b, kv_norm_g, w_kv_b, w_o):
    proj = h @ w_in
    o1 = 3 * C_WIDTH
    gb, gc, xc, q_lat, kv_lat, k_rope = jnp.split(
        proj, [C_WIDTH, 2 * C_WIDTH, o1, o1 + D_Q_RANK, o1 + D_Q_RANK + D_KV_RANK], axis=-1)
    c_out = short_conv_mixer(gb, gc, xc, conv_w)
    d_out = mla_attention(q_lat, kv_lat, k_rope, q_norm_g, w_q_b, kv_norm_g, w_kv_b)
    return jnp.concatenate([c_out, d_out], axis=-1) @ w_o


def swiglu(h, w_gate, w_up, w_down):
    return (jax.nn.silu(h @ w_gate) * (h @ w_up)) @ w_down


def moe_swiglu(h, w_router, b_router, w_gate, w_up, w_down):
    Bn, S, Dm = h.shape
    xt = h.reshape(-1, Dm)
    N = xt.shape[0]
    NA = N * TOP_K
    logits = (xt @ w_router).astype(jnp.float32) + b_router.astype(jnp.float32)
    top_logit, top_e = lax.top_k(logits, TOP_K)
    gates = jax.nn.softmax(top_logit, axis=-1).astype(h.dtype)
    flat_e = top_e.reshape(-1).astype(jnp.int32)
    order = jnp.argsort(flat_e, stable=True)
    sorted_e = flat_e[order]
    counts = jnp.bincount(flat_e, length=N_EXPERTS).astype(jnp.int32)
    padded = (counts + MOE_BLOCK - 1) // MOE_BLOCK * MOE_BLOCK
    pad_end = jnp.cumsum(padded)
    pad_start = pad_end - padded
    grp_start = jnp.cumsum(counts) - counts
    dest_sorted = pad_start[sorted_e] + jnp.arange(NA, dtype=jnp.int32) - grp_start[sorted_e]
    dest = jnp.zeros((NA,), jnp.int32).at[order].set(dest_sorted)
    n_blk = -(-NA // MOE_BLOCK) + N_EXPERTS
    L = n_blk * MOE_BLOCK
    tok = jnp.arange(NA, dtype=jnp.int32) // TOP_K
    slot_tok = jnp.zeros((L,), jnp.int32).at[dest].set(tok)
    xs = xt[slot_tok].reshape(n_blk, MOE_BLOCK, Dm)
    blk_e = jnp.minimum(jnp.searchsorted(pad_end, jnp.arange(n_blk, dtype=jnp.int32) * MOE_BLOCK, side='right'),
                        N_EXPERTS - 1)

    def expert_block(args):
        xb, e = args
        return (jax.nn.silu(xb @ w_gate[e]) * (xb @ w_up[e])) @ w_down[e]

    ys = lax.map(expert_block, (xs, blk_e)).reshape(L, Dm)
    y = ys[dest].reshape(N, TOP_K, Dm)
    out = jnp.einsum('nk,nkd->nd', gates, y)
    return out.reshape(Bn, S, Dm)


def setup_inputs(seed: int = 0) -> dict:
    key = jax.random.key(seed)
    ks = iter(jax.random.split(key, 40))
    D = D_MODEL
    E_, O_ = N_EVEN, N_ODD

    def nrm(shape, scale):
        return jax.random.normal(next(ks), shape, jnp.float32) * scale

    def gain(shape):
        return 1.0 + nrm(shape, 0.02)

    return {
        'x': nrm((BATCH, SEQ, D), 1.0),
        'c': nrm((BATCH, D), 1.0),
        'even_ada_w': nrm((E_, D, ADA_CHUNKS * D), 0.5 * D ** -0.5),
        'even_ada_b': nrm((E_, ADA_CHUNKS * D), 0.01),
        'even_norm_mix_g': gain((E_, D)),
        'even_w_in': nrm((E_, D, EVEN_IN), D ** -0.5),
        'even_sinks': nrm((E_, A_HEADS), 0.5),
        'even_gmlp_ln_g': gain((E_, B_WIDTH)),
        'even_gmlp_ln_b': nrm((E_, B_WIDTH), 0.02),
        'even_w_s': nrm((E_, B_GROUPS, B_CHUNK, B_CHUNK), B_CHUNK ** -0.5),
        'even_b_s': 1.0 + nrm((E_, B_GROUPS, B_CHUNK), 0.1),
        'even_w_o': nrm((E_, EVEN_MIX, D), EVEN_MIX ** -0.5),
        'even_norm_ffn_g': gain((E_, D)),
        'even_ffn_w_gate': nrm((E_, D, FF_DENSE), D ** -0.5),
        'even_ffn_w_up': nrm((E_, D, FF_DENSE), D ** -0.5),
        'even_ffn_w_down': nrm((E_, FF_DENSE, D), FF_DENSE ** -0.5),
        'odd_ada_w': nrm((O_, D, ADA_CHUNKS * D), 0.5 * D ** -0.5),
        'odd_ada_b': nrm((O_, ADA_CHUNKS * D), 0.01),
        'odd_norm_mix_g': gain((O_, D)),
        'odd_w_in': nrm((O_, D, ODD_IN), D ** -0.5),
        'odd_conv_w': nrm((O_, C_CONV, C_WIDTH), C_CONV ** -0.5),
        'odd_q_norm_g': gain((O_, D_Q_RANK)),
        'odd_w_q_b': nrm((O_, D_Q_RANK, D_HEADS * (D_NOPE + D_ROPE)), D_Q_RANK ** -0.5),
        'odd_kv_norm_g': gain((O_, D_KV_RANK)),
        'odd_w_kv_b': nrm((O_, D_KV_RANK, D_HEADS * (D_NOPE + D_VDIM)), D_KV_RANK ** -0.5),
        'odd_w_o': nrm((O_, ODD_MIX, D), ODD_MIX ** -0.5),
        'odd_norm_ffn_g': gain((O_, D)),
        'odd_router_w': nrm((O_, D, N_EXPERTS), D ** -0.5),
        'odd_router_b': nrm((O_, N_EXPERTS), 0.01),
        'odd_exp_w_gate': nrm((O_, N_EXPERTS, D, FF_EXPERT), D ** -0.5),
        'odd_exp_w_up': nrm((O_, N_EXPERTS, D, FF_EXPERT), D ** -0.5),
        'odd_exp_w_down': nrm((O_, N_EXPERTS, FF_EXPERT, D), FF_EXPERT ** -0.5),
        'final_norm_g': gain((D,)),
    }


def reference(x, c, even_ada_w, even_ada_b, even_norm_mix_g, even_w_in, even_sinks, even_gmlp_ln_g,
              even_gmlp_ln_b, even_w_s, even_b_s, even_w_o, even_norm_ffn_g, even_ffn_w_gate, even_ffn_w_up,
              even_ffn_w_down, odd_ada_w, odd_ada_b, odd_norm_mix_g, odd_w_in, odd_conv_w, odd_q_norm_g,
              odd_w_q_b, odd_kv_norm_g, odd_w_kv_b, odd_w_o, odd_norm_ffn_g, odd_router_w, odd_router_b,
              odd_exp_w_gate, odd_exp_w_up, odd_exp_w_down, final_norm_g):
    cond = jax.nn.silu(c)
    h = x
    for layer in range(DEPTH):
        i = layer // 2
        if layer % 2 == 0:
            mod = cond @ even_ada_w[i] + even_ada_b[i]
            sh1, sc1, g1, sh2, sc2, g2 = jnp.split(mod, ADA_CHUNKS, axis=-1)
            hn = modulate(h, even_norm_mix_g[i], sh1, sc1)
            h = h + g1[:, None, :] * even_mixer(hn, even_w_in[i], even_sinks[i], even_gmlp_ln_g[i],
                                                even_gmlp_ln_b[i], even_w_s[i], even_b_s[i], even_w_o[i])
            hn = modulate(h, even_norm_ffn_g[i], sh2, sc2)
            h = h + g2[:, None, :] * swiglu(hn, even_ffn_w_gate[i], even_ffn_w_up[i], even_ffn_w_down[i])
        else:
            mod = cond @ odd_ada_w[i] + odd_ada_b[i]
            sh1, sc1, g1, sh2, sc2, g2 = jnp.split(mod, ADA_CHUNKS, axis=-1)
            hn = modulate(h, odd_norm_mix_g[i], sh1, sc1)
            h = h + g1[:, None, :] * odd_mixer(hn, odd_w_in[i], odd_conv_w[i], odd_q_norm_g[i], odd_w_q_b[i],
                                               odd_kv_norm_g[i], odd_w_kv_b[i], odd_w_o[i])
            hn = modulate(h, odd_norm_ffn_g[i], sh2, sc2)
            h = h + g2[:, None, :] * moe_swiglu(hn, odd_router_w[i], odd_router_b[i], odd_exp_w_gate[i],
                                                odd_exp_w_up[i], odd_exp_w_down[i])
    return rmsnorm(h, final_norm_g)
```

```python
import functools

import jax
import jax.numpy as jnp
import numpy as np
from jax import lax
from jax.experimental import pallas as pl
from jax.experimental.pallas import tpu as pltpu

F32 = jnp.float32
BF16 = jnp.bfloat16

D_MODEL = 1024
NORM_EPS = 1e-6
ADA_CHUNKS = 6
BLOCK = 128

A_HEADS = 8
A_KV_HEADS = 2
A_GROUP = A_HEADS // A_KV_HEADS
A_HEAD_DIM = 64
A_Q = A_HEADS * A_HEAD_DIM
A_KV = A_KV_HEADS * A_HEAD_DIM

B_GROUPS = 8
B_GROUP_DIM = 64
B_WIDTH = B_GROUPS * B_GROUP_DIM
B_CHUNK = 128

C_WIDTH = 512
C_CONV = 3

D_HEADS = 8
D_NOPE = 64
D_ROPE = 32
D_VDIM = 64
D_Q_RANK = 512
D_KV_RANK = 256
ROPE_THETA = 10000.0
D_HEAD_PAD = 128

N_EXPERTS = 8
TOP_K = 2
MOE_BLOCK = 512

LANES = 128
NEG_BIG = -1e30
VMEM_LIMIT = 56 * 1024 * 1024


def _params(*sem):
    return pltpu.CompilerParams(dimension_semantics=sem, vmem_limit_bytes=VMEM_LIMIT)


def _dot(a, b):
    return jnp.dot(a, b, preferred_element_type=F32)


def _dot_nt(a, b):
    return lax.dot_general(a, b, (((1,), (1,)), ((), ())), preferred_element_type=F32)


def _modulated_norm(x, g, scale, shift):
    ms = jnp.mean(x * x, axis=-1, keepdims=True)
    return (x * lax.rsqrt(ms + NORM_EPS)) * (g * (1.0 + scale)) + shift


def _gelu(x):
    return 0.5 * x * (1.0 + lax.erf(x * np.float32(0.7071067811865476)))


def _silu(x):
    return x * (1.0 / (1.0 + jnp.exp(-x)))


def _ada_kernel(c_ref, w_ref, b_ref, o_ref):
    cond = _silu(c_ref[...])
    o_ref[...] = _dot(cond.astype(BF16), w_ref[...].astype(BF16)) + b_ref[...]


def _ada(c, w, b):
    bn, d = c.shape
    rows = 8
    tn = 1536
    cp = jnp.zeros((rows, d), F32).at[:bn].set(c)
    out = pl.pallas_call(
        _ada_kernel,
        grid=(w.shape[1] // tn,),
        in_specs=[pl.BlockSpec((rows, d), lambda j: (0, 0)),
                  pl.BlockSpec((d, tn), lambda j: (0, j)),
                  pl.BlockSpec((1, tn), lambda j: (0, j))],
        out_specs=pl.BlockSpec((rows, tn), lambda j: (0, j)),
        out_shape=jax.ShapeDtypeStruct((rows, w.shape[1]), F32),
        compiler_params=_params("arbitrary"),
        name="ada",
    )(cp, w, b.reshape(1, -1))
    return out[:bn].reshape(bn * ADA_CHUNKS, 1, d)


def _mod_spec(chunk, tiles_per_seq):
    return pl.BlockSpec((1, 1, D_MODEL), lambda i, *_: ((i // tiles_per_seq) * ADA_CHUNKS + chunk, 0, 0))


def _even_in_kernel(x_ref, g_ref, sh_ref, sc_ref, w_ref, o_ref):
    hn = _modulated_norm(x_ref[...], g_ref[...], sc_ref[0], sh_ref[0])
    o_ref[...] = _dot(hn.astype(BF16), w_ref[...]).astype(o_ref.dtype)


def _even_in(h, mod, g, w, seq, tm=512):
    n, d = h.shape
    tps = seq // tm
    nout = w.shape[1]
    return pl.pallas_call(
        _even_in_kernel,
        grid=(n // tm,),
        in_specs=[pl.BlockSpec((tm, d), lambda i: (i, 0)),
                  pl.BlockSpec((1, d), lambda i: (0, 0)),
                  _mod_spec(0, tps), _mod_spec(1, tps),
                  pl.BlockSpec((d, nout), lambda i: (0, 0))],
        out_specs=pl.BlockSpec((tm, nout), lambda i: (i, 0)),
        out_shape=jax.ShapeDtypeStruct((n, nout), BF16),
        compiler_params=_params("arbitrary"),
        name="even_in",
    )(h, g.reshape(1, d), mod, mod, w)


EVEN_TQ = 512


def _even_core_kernel(q_ref, u_ref, gg_ref, kv_ref, kvp_ref, sinks_ref, lng_ref, lnb_ref, ws_ref, bst_ref,
                      a_ref, b_ref, *, tiles_per_seq):
    first_key = jnp.where((pl.program_id(0) % tiles_per_seq) == 0, BLOCK, 0)
    nsub = EVEN_TQ // BLOCK

    qi = lax.broadcasted_iota(jnp.int32, (BLOCK, 2 * BLOCK), 0)
    kj = lax.broadcasted_iota(jnp.int32, (BLOCK, 2 * BLOCK), 1)
    dist = qi + BLOCK - kj
    valid = (dist >= 0) & (dist < BLOCK)
    valid_first = valid & (kj >= first_key)
    distf = dist.astype(F32)
    for sb in range(nsub):
        rows = slice(sb * BLOCK, (sb + 1) * BLOCK)
        cur = kv_ref[rows, :]
        prev = kvp_ref[...] if sb == 0 else kv_ref[(sb - 1) * BLOCK:sb * BLOCK, :]
        band = jnp.concatenate([prev, cur], axis=0)
        vmask = valid_first if sb == 0 else valid
        outs = []
        for h in range(A_HEADS):
            kvh = h // A_GROUP
            kh = band[:, kvh * A_HEAD_DIM:(kvh + 1) * A_HEAD_DIM]
            vh = band[:, A_KV + kvh * A_HEAD_DIM:A_KV + (kvh + 1) * A_HEAD_DIM]
            qh = q_ref[rows, h * A_HEAD_DIM:(h + 1) * A_HEAD_DIM]
            s = _dot_nt(qh, kh) * np.float32(A_HEAD_DIM ** -0.5) - np.float32(2.0 ** -(h + 1)) * distf
            s = jnp.where(vmask, s, NEG_BIG)
            sink = sinks_ref[h]
            m = jnp.maximum(jnp.max(s, axis=-1, keepdims=True), sink)
            p = jnp.exp(s - m)
            denom = jnp.sum(p, axis=-1, keepdims=True) + jnp.exp(sink - m)
            outs.append(_dot(p.astype(BF16), vh) / denom)
        a_ref[rows, :] = jnp.concatenate(outs, axis=1).astype(a_ref.dtype)

    gu = _gelu(u_ref[...].astype(F32))
    gv = _gelu(gg_ref[...].astype(F32))
    mu = jnp.mean(gv, axis=-1, keepdims=True)
    cen = gv - mu
    var = jnp.mean(cen * cen, axis=-1, keepdims=True)
    vn = ((cen * lax.rsqrt(var + NORM_EPS)) * lng_ref[...] + lnb_ref[...]).astype(BF16)
    tt = lax.broadcasted_iota(jnp.int32, (B_CHUNK, B_CHUNK), 0)
    ss = lax.broadcasted_iota(jnp.int32, (B_CHUNK, B_CHUNK), 1)
    causal = ss <= tt
    wm = [jnp.where(causal, ws_ref[gp], 0.0).astype(BF16) for gp in range(B_GROUPS)]
    for c in range(EVEN_TQ // B_CHUNK):
        rows = slice(c * B_CHUNK, (c + 1) * B_CHUNK)
        cols = []
        for gp in range(B_GROUPS):
            lanes = slice(gp * B_GROUP_DIM, (gp + 1) * B_GROUP_DIM)
            cols.append(_dot(wm[gp], vn[rows, lanes]) + bst_ref[:, gp:gp + 1])
        b_ref[rows, :] = (gu[rows, :] * jnp.concatenate(cols, axis=1)).astype(b_ref.dtype)


def _even_core(proj, sinks, ln_g, ln_b, w_s, b_s, seq):
    n = proj.shape[0]
    tq = EVEN_TQ
    tps = seq // tq
    sub = tq // BLOCK
    kvb = (A_Q + 2 * B_WIDTH) // (2 * A_KV)
    kern = functools.partial(_even_core_kernel, tiles_per_seq=tps)
    return pl.pallas_call(
        kern,
        grid=(n // tq,),
        in_specs=[pl.BlockSpec((tq, A_Q), lambda i: (i, 0)),
                  pl.BlockSpec((tq, B_WIDTH), lambda i: (i, 1)),
                  pl.BlockSpec((tq, B_WIDTH), lambda i: (i, 2)),
                  pl.BlockSpec((tq, 2 * A_KV), lambda i: (i, kvb)),
                  pl.BlockSpec((BLOCK, 2 * A_KV), lambda i: (jnp.maximum(i * sub - 1, 0), kvb)),
                  pl.BlockSpec(memory_space=pltpu.SMEM),
                  pl.BlockSpec((1, B_WIDTH), lambda i: (0, 0)),
                  pl.BlockSpec((1, B_WIDTH), lambda i: (0, 0)),
                  pl.BlockSpec((B_GROUPS, B_CHUNK, B_CHUNK), lambda i: (0, 0, 0)),
                  pl.BlockSpec((B_CHUNK, B_GROUPS), lambda i: (0, 0))],
        out_specs=[pl.BlockSpec((tq, A_Q), lambda i: (i, 0)),
                   pl.BlockSpec((tq, B_WIDTH), lambda i: (i, 0))],
        out_shape=[jax.ShapeDtypeStruct((n, A_Q), BF16), jax.ShapeDtypeStruct((n, B_WIDTH), BF16)],
        compiler_params=_params("arbitrary"),
        name="even_core",
    )(proj, proj, proj, proj, proj, sinks, ln_g.reshape(1, -1), ln_b.reshape(1, -1), w_s, b_s.T)


def _out_proj_kernel(m1_ref, m2_ref, h_ref, gate_ref, w_ref, o_ref):
    half = m1_ref.shape[1]
    y = _dot(m1_ref[...], w_ref[0:half, :]) + _dot(m2_ref[...], w_ref[half:2 * half, :])
    o_ref[...] = h_ref[...] + gate_ref[0] * y


def _out_proj(m1, m2, h, mod, w, seq, tm=512):
    n, d = h.shape
    tps = seq // tm
    half = m1.shape[1]
    return pl.pallas_call(
        _out_proj_kernel,
        grid=(n // tm,),
        in_specs=[pl.BlockSpec((tm, half), lambda i: (i, 0)),
                  pl.BlockSpec((tm, half), lambda i: (i, 0)),
                  pl.BlockSpec((tm, d), lambda i: (i, 0)),
                  _mod_spec(2, tps),
                  pl.BlockSpec((2 * half, d), lambda i: (0, 0))],
        out_specs=pl.BlockSpec((tm, d), lambda i: (i, 0)),
        out_shape=jax.ShapeDtypeStruct((n, d), F32),
        compiler_params=_params("arbitrary"),
        name="out_proj",
    )(m1, m2, h, mod, w)


def _ffn_kernel(h_ref, g_ref, sh_ref, sc_ref, gate_ref, wg_ref, wu_ref, wd_ref, o_ref, hn_ref, acc_ref):
    c = pl.program_id(1)

    @pl.when(c == 0)
    def _():
        hn_ref[...] = _modulated_norm(h_ref[...], g_ref[...], sc_ref[0], sh_ref[0]).astype(BF16)
        acc_ref[...] = jnp.zeros_like(acc_ref)

    hn = hn_ref[...]
    a = _dot(hn, wg_ref[...])
    b = _dot(hn, wu_ref[...])
    acc_ref[...] += _dot((_silu(a) * b).astype(BF16), wd_ref[...])

    @pl.when(c == pl.num_programs(1) - 1)
    def _():
        o_ref[...] = h_ref[...] + gate_ref[0] * acc_ref[...]


def _ffn(h, mod, g, wg, wu, wd, seq, tm=1024, ch=256):
    n, d = h.shape
    ff = wg.shape[1]
    tps = seq // tm
    return pl.pallas_call(
        _ffn_kernel,
        grid=(n // tm, ff // ch),
        in_specs=[pl.BlockSpec((tm, d), lambda i, c: (i, 0)),
                  pl.BlockSpec((1, d), lambda i, c: (0, 0)),
                  _mod_spec(3, tps), _mod_spec(4, tps), _mod_spec(5, tps),
                  pl.BlockSpec((d, ch), lambda i, c: (0, c)),
                  pl.BlockSpec((d, ch), lambda i, c: (0, c)),
                  pl.BlockSpec((ch, d), lambda i, c: (c, 0))],
        out_specs=pl.BlockSpec((tm, d), lambda i, c: (i, 0)),
        out_shape=jax.ShapeDtypeStruct((n, d), F32),
        scratch_shapes=[pltpu.VMEM((tm, d), BF16), pltpu.VMEM((tm, d), F32)],
        compiler_params=_params("arbitrary", "arbitrary"),
        name="ffn",
    )(h, g.reshape(1, d), mod, mod, mod, wg, wu, wd)


def _odd_in_kernel(x_ref, g_ref, sh_ref, sc_ref, w_ref, cw_ref, qg_ref, wqa_ref, wqb_ref, kg_ref, wkk_ref, wkv_ref,
                   cos_ref, sin_ref, c_ref, q_ref, k_ref, v_ref, carry_ref, *, tiles_per_seq):
    first = (pl.program_id(0) % tiles_per_seq) == 0
    tm = x_ref.shape[0]
    hn = _modulated_norm(x_ref[...], g_ref[...], sc_ref[0], sh_ref[0]).astype(BF16)
    proj = _dot(hn, w_ref[...])
    o = 0
    gb = proj[:, o:o + C_WIDTH]; o += C_WIDTH
    gc = proj[:, o:o + C_WIDTH]; o += C_WIDTH
    xc = proj[:, o:o + C_WIDTH]; o += C_WIDTH
    q_lat = proj[:, o:o + D_Q_RANK]; o += D_Q_RANK
    kv_lat = proj[:, o:o + D_KV_RANK]; o += D_KV_RANK
    kr_a = proj[:, o:o + LANES]; o += LANES
    kr_b = proj[:, o:o + LANES]

    @pl.when(first)
    def _():
        carry_ref[...] = jnp.zeros_like(carry_ref)

    z = gc * xc
    prev1 = carry_ref[7:8, :]
    prev2 = carry_ref[6:7, :]
    row = lax.broadcasted_iota(jnp.int32, z.shape, 0)
    z1 = jnp.where(row == 0, prev1, pltpu.roll(z, 1, 0))
    z2 = jnp.where(row == 0, prev2, jnp.where(row == 1, prev1, pltpu.roll(z, 2, 0)))
    y = cw_ref[0:1, :] * z2 + cw_ref[1:2, :] * z1 + cw_ref[2:3, :] * z
    c_ref[...] = (gb * y).astype(c_ref.dtype)
    carry_ref[...] = z[tm - 8:tm, :]

    cosf = jnp.concatenate([cos_ref[...]] * D_HEADS, axis=1)
    sinf = jnp.concatenate([sin_ref[...]] * D_HEADS, axis=1)
    qn = (q_lat * lax.rsqrt(jnp.mean(q_lat * q_lat, axis=-1, keepdims=True) + NORM_EPS) * qg_ref[...]).astype(BF16)
    q_ref[...] = (_dot(qn, wqa_ref[...]) * cosf + _dot(qn, wqb_ref[...]) * sinf).astype(q_ref.dtype)
    kn = (kv_lat * lax.rsqrt(jnp.mean(kv_lat * kv_lat, axis=-1, keepdims=True) + NORM_EPS) * kg_ref[...]).astype(BF16)
    kpe = kr_a * cos_ref[...] + kr_b * sin_ref[...]
    k_ref[...] = (_dot(kn, wkk_ref[...]) + jnp.concatenate([kpe] * D_HEADS, axis=1)).astype(k_ref.dtype)
    v_ref[...] = _dot(kn, wkv_ref[...]).astype(v_ref.dtype)


def _odd_in(h, mod, g, w, conv_w, qg, wqa, wqb, kg, wkk, wkv, cos_t, sin_t, seq, tm=512):
    n, d = h.shape
    tps = seq // tm
    hw = D_HEADS * D_HEAD_PAD
    const = lambda a: pl.BlockSpec(a.shape, lambda i: (0,) * a.ndim)
    qg = qg.reshape(1, -1)
    kg = kg.reshape(1, -1)
    kern = functools.partial(_odd_in_kernel, tiles_per_seq=tps)
    return pl.pallas_call(
        kern,
        grid=(n // tm,),
        in_specs=[pl.BlockSpec((tm, d), lambda i: (i, 0)),
                  pl.BlockSpec((1, d), lambda i: (0, 0)),
                  _mod_spec(0, tps), _mod_spec(1, tps),
                  const(w), const(conv_w), const(qg), const(wqa), const(wqb), const(kg), const(wkk), const(wkv),
                  pl.BlockSpec((tm, LANES), lambda i: (i % tps, 0)),
                  pl.BlockSpec((tm, LANES), lambda i: (i % tps, 0))],
        out_specs=[pl.BlockSpec((tm, C_WIDTH), lambda i: (i, 0)),
                   pl.BlockSpec((tm, hw), lambda i: (i, 0)),
                   pl.BlockSpec((tm, hw), lambda i: (i, 0)),
                   pl.BlockSpec((tm, D_HEADS * D_VDIM), lambda i: (i, 0))],
        out_shape=[jax.ShapeDtypeStruct((n, C_WIDTH), BF16),
                   jax.ShapeDtypeStruct((n, hw), BF16),
                   jax.ShapeDtypeStruct((n, hw), BF16),
                   jax.ShapeDtypeStruct((n, D_HEADS * D_VDIM), BF16)],
        scratch_shapes=[pltpu.VMEM((8, C_WIDTH), F32)],
        compiler_params=_params("arbitrary"),
        name="odd_in",
    )(h, g.reshape(1, d), mod, mod, w, conv_w, qg, wqa, wqb, kg, wkk, wkv, cos_t, sin_t)


MLA_T = 512


def _mla_kernel(qi_ref, kj_ref, q_ref, k_ref, v_ref, o_ref, m_ref, l_ref, acc_ref):
    t = pl.program_id(1)
    qi = qi_ref[t]
    kj = kj_ref[t]
    tq, tk = MLA_T, MLA_T
    rep = tk // LANES

    @pl.when(kj == 0)
    def _():
        m_ref[...] = jnp.full_like(m_ref, NEG_BIG)
        l_ref[...] = jnp.zeros_like(l_ref)
        acc_ref[...] = jnp.zeros_like(acc_ref)

    row = lax.broadcasted_iota(jnp.int32, (tq, tk), 0)
    col = lax.broadcasted_iota(jnp.int32, (tq, tk), 1)
    keep = col <= row + jnp.where(kj < qi, tk, 0)
    scale = np.float32((D_NOPE + D_ROPE) ** -0.5)
    for h in range(D_HEADS):
        hl = slice(h * D_HEAD_PAD, (h + 1) * D_HEAD_PAD)
        s = _dot_nt(q_ref[:, hl], k_ref[:, hl]) * scale
        s = jnp.where(keep, s, NEG_BIG)
        m_prev = m_ref[h]
        m_next = jnp.maximum(m_prev, jnp.max(s, axis=-1, keepdims=True))
        alpha = jnp.exp(m_prev - m_next)
        p = jnp.exp(s - jnp.concatenate([m_next] * rep, axis=1))
        l_ref[h] = alpha * l_ref[h] + jnp.sum(p, axis=-1, keepdims=True)
        m_ref[h] = m_next
        vp = v_ref[:, (h // 2) * LANES:(h // 2 + 1) * LANES]
        acc_ref[h] = alpha * acc_ref[h] + _dot(p.astype(BF16), vp)

    @pl.when(kj == qi)
    def _():
        lane = lax.broadcasted_iota(jnp.int32, (tq, LANES), 1)
        for hp in range(D_HEADS // 2):
            lo = acc_ref[2 * hp] / l_ref[2 * hp]
            hi = acc_ref[2 * hp + 1] / l_ref[2 * hp + 1]
            o_ref[:, hp * LANES:(hp + 1) * LANES] = jnp.where(lane < D_VDIM, lo, hi).astype(o_ref.dtype)


def _mla(qf, kf, v, batch, seq):
    n = qf.shape[0]
    t = MLA_T
    nq = seq // t
    pairs = [(i, j) for i in range(nq) for j in range(i + 1)]
    qi = jnp.asarray([p[0] for p in pairs], jnp.int32)
    kj = jnp.asarray([p[1] for p in pairs], jnp.int32)
    hw = D_HEADS * D_HEAD_PAD
    vw = D_HEADS * D_VDIM
    grid_spec = pltpu.PrefetchScalarGridSpec(
        num_scalar_prefetch=2,
        grid=(batch, len(pairs)),
        in_specs=[pl.BlockSpec((t, hw), lambda b, s, qi, kj: (b * nq + qi[s], 0)),
                  pl.BlockSpec((t, hw), lambda b, s, qi, kj: (b * nq + kj[s], 0)),
                  pl.BlockSpec((t, vw), lambda b, s, qi, kj: (b * nq + kj[s], 0))],
        out_specs=pl.BlockSpec((t, vw), lambda b, s, qi, kj: (b * nq + qi[s], 0)),
        scratch_shapes=[pltpu.VMEM((D_HEADS, t, LANES), F32),
                        pltpu.VMEM((D_HEADS, t, LANES), F32),
                        pltpu.VMEM((D_HEADS, t, LANES), F32)],
    )
    return pl.pallas_call(
        _mla_kernel,
        grid_spec=grid_spec,
        out_shape=jax.ShapeDtypeStruct((n, vw), BF16),
        compiler_params=_params("arbitrary", "arbitrary"),
        name="mla",
    )(qi, kj, qf, kf, v)


ROUTER_TM = 512


def _router_kernel(h_ref, g_ref, sh_ref, sc_ref, wr_ref, br_ref, hn_ref, eid_ref, gate_ref, rank_ref, cnt_ref,
                   carry_ref):
    tm = h_ref.shape[0]

    @pl.when(pl.program_id(0) == 0)
    def _():
        carry_ref[...] = jnp.zeros_like(carry_ref)

    hn32 = _modulated_norm(h_ref[...], g_ref[...], sc_ref[0], sh_ref[0])
    hn_ref[...] = hn32
    hn = hn32.astype(BF16)
    lane = lax.broadcasted_iota(jnp.int32, (tm, LANES), 1)
    logits = jnp.where(lane < N_EXPERTS, _dot(hn, wr_ref[...]) + br_ref[...], NEG_BIG)
    m1 = jnp.max(logits, axis=-1, keepdims=True)
    e1 = jnp.min(jnp.where(logits == m1, lane, LANES), axis=-1, keepdims=True)
    rest = jnp.where(lane == e1, NEG_BIG, logits)
    m2 = jnp.max(rest, axis=-1, keepdims=True)
    e2 = jnp.min(jnp.where(rest == m2, lane, LANES), axis=-1, keepdims=True)
    ex = jnp.exp(m2 - m1)
    g1 = 1.0 / (1.0 + ex)
    g2 = ex / (1.0 + ex)
    eid_ref[...] = jnp.where(lane == 0, e1, jnp.where(lane == 1, e2, 0))
    gate_ref[...] = jnp.where(lane == 0, g1, jnp.where(lane == 1, g2, 0.0))
    onehot = ((lane == e1) | (lane == e2)).astype(BF16)
    tt = lax.broadcasted_iota(jnp.int32, (tm, tm), 0)
    ss = lax.broadcasted_iota(jnp.int32, (tm, tm), 1)
    before = (ss < tt).astype(BF16)
    prefix = _dot(before, onehot) + carry_ref[...]
    r1 = jnp.sum(jnp.where(lane == e1, prefix, 0.0), axis=-1, keepdims=True)
    r2 = jnp.sum(jnp.where(lane == e2, prefix, 0.0), axis=-1, keepdims=True)
    rank_ref[...] = jnp.where(lane == 0, r1, jnp.where(lane == 1, r2, 0.0)).astype(jnp.int32)
    total = carry_ref[...] + jnp.sum(onehot.astype(F32), axis=0, keepdims=True)
    carry_ref[...] = total
    cnt_ref[...] = jnp.broadcast_to(total, cnt_ref.shape).astype(jnp.int32)


def _router(h, mod, g, wr, br, seq):
    n, d = h.shape
    tm = ROUTER_TM
    tps = seq // tm
    wide = lambda dt: jax.ShapeDtypeStruct((n, LANES), dt)
    row_spec = pl.BlockSpec((tm, LANES), lambda i: (i, 0))
    return pl.pallas_call(
        _router_kernel,
        grid=(n // tm,),
        in_specs=[pl.BlockSpec((tm, d), lambda i: (i, 0)),
                  pl.BlockSpec((1, d), lambda i: (0, 0)),
                  _mod_spec(3, tps), _mod_spec(4, tps),
                  pl.BlockSpec((d, LANES), lambda i: (0, 0)),
                  pl.BlockSpec((1, LANES), lambda i: (0, 0))],
        out_specs=[pl.BlockSpec((tm, d), lambda i: (i, 0)), row_spec, row_spec, row_spec,
                   pl.BlockSpec((8, LANES), lambda i: (0, 0))],
        out_shape=[jax.ShapeDtypeStruct((n, d), F32), wide(jnp.int32), wide(F32), wide(jnp.int32),
                   jax.ShapeDtypeStruct((8, LANES), jnp.int32)],
        scratch_shapes=[pltpu.VMEM((1, LANES), F32)],
        compiler_params=_params("arbitrary"),
        name="router",
    )(h, g.reshape(1, d), mod, mod, wr, br)


GATHER_ROWS = 512


def _row_copy(src_ref, dst_ref, src_row, dst_row, sem):
    return pltpu.make_async_copy(src_ref.at[pl.ds(src_row, 1)], dst_ref.at[pl.ds(dst_row, 1)], sem)


def _gather_kernel(idx_ref, src_ref, o_ref, sem):
    def issue(r, carry):
        _row_copy(src_ref, o_ref, idx_ref[0, 0, r], r, sem).start()
        return carry

    lax.fori_loop(0, GATHER_ROWS, issue, 0)
    pltpu.make_async_copy(src_ref.at[pl.ds(0, GATHER_ROWS)], o_ref, sem).wait()


def _gather_rows(src, idx):
    w = src.shape[1]
    l = idx.shape[0]
    nb = l // GATHER_ROWS
    return pl.pallas_call(
        _gather_kernel,
        grid=(nb,),
        in_specs=[pl.BlockSpec((1, 1, GATHER_ROWS), lambda i: (i, 0, 0), memory_space=pltpu.SMEM),
                  pl.BlockSpec(memory_space=pl.ANY)],
        out_specs=pl.BlockSpec((GATHER_ROWS, w), lambda i: (i, 0)),
        out_shape=jax.ShapeDtypeStruct((l, w), src.dtype),
        scratch_shapes=[pltpu.SemaphoreType.DMA(())],
        compiler_params=_params("arbitrary"),
        name="dispatch_gather",
    )(idx.reshape(nb, 1, GATHER_ROWS), src)


def _expert_kernel(be_ref, nu_ref, x_ref, wg_ref, wu_ref, wd_ref, o_ref, acc_ref):
    b = pl.program_id(0)
    c = pl.program_id(1)
    used = b < nu_ref[0]

    @pl.when(c == 0)
    def _():
        acc_ref[...] = jnp.zeros_like(acc_ref)

    @pl.when(used)
    def _():
        x = x_ref[...].astype(BF16)
        a = _dot(x, wg_ref[0])
        u = _dot(x, wu_ref[0])
        acc_ref[...] += _dot((_silu(a) * u).astype(BF16), wd_ref[0])

    @pl.when(c == pl.num_programs(1) - 1)
    def _():
        o_ref[...] = acc_ref[...]


def _experts(xs, blk_e, n_used, wg, wu, wd, ch=512):
    l, d = xs.shape
    ff = wg.shape[2]
    bm = MOE_BLOCK
    grid_spec = pltpu.PrefetchScalarGridSpec(
        num_scalar_prefetch=2,
        grid=(l // bm, ff // ch),
        in_specs=[pl.BlockSpec((bm, d), lambda b, c, be, nu: (b, 0)),
                  pl.BlockSpec((1, d, ch), lambda b, c, be, nu: (be[b], 0, c)),
                  pl.BlockSpec((1, d, ch), lambda b, c, be, nu: (be[b], 0, c)),
                  pl.BlockSpec((1, ch, d), lambda b, c, be, nu: (be[b], c, 0))],
        out_specs=pl.BlockSpec((bm, d), lambda b, c, be, nu: (b, 0)),
        scratch_shapes=[pltpu.VMEM((bm, d), F32)],
    )
    return pl.pallas_call(
        _expert_kernel,
        grid_spec=grid_spec,
        out_shape=jax.ShapeDtypeStruct((l, d), F32),
        compiler_params=_params("arbitrary", "arbitrary"),
        name="experts",
    )(blk_e, n_used, xs, wg, wu, wd)


COMBINE_TM = 256


def _combine_kernel(d0_ref, d1_ref, ys_ref, h_ref, gates_ref, gate_ref, fg_ref, o_ref, buf0, buf1, sem):
    tm = COMBINE_TM

    def issue(r, carry):
        _row_copy(ys_ref, buf0, d0_ref[0, 0, r], r, sem.at[0]).start()
        _row_copy(ys_ref, buf1, d1_ref[0, 0, r], r, sem.at[1]).start()
        return carry

    lax.fori_loop(0, tm, issue, 0)
    pltpu.make_async_copy(ys_ref.at[pl.ds(0, tm)], buf0, sem.at[0]).wait()
    pltpu.make_async_copy(ys_ref.at[pl.ds(0, tm)], buf1, sem.at[1]).wait()
    moe = gates_ref[:, 0:1] * buf0[...] + gates_ref[:, 1:2] * buf1[...]
    hh = h_ref[...] + gate_ref[0] * moe
    r = lax.rsqrt(jnp.mean(hh * hh, axis=-1, keepdims=True) + NORM_EPS)
    o_ref[...] = (hh * r) * fg_ref[...]


def _combine(ys, dest, gates, h, mod, final_g, seq):
    n, d = h.shape
    tm = COMBINE_TM
    tps = seq // tm
    nb = n // tm
    idx_spec = pl.BlockSpec((1, 1, tm), lambda i: (i, 0, 0), memory_space=pltpu.SMEM)
    return pl.pallas_call(
        _combine_kernel,
        grid=(nb,),
        in_specs=[idx_spec, idx_spec,
                  pl.BlockSpec(memory_space=pl.ANY),
                  pl.BlockSpec((tm, d), lambda i: (i, 0)),
                  pl.BlockSpec((tm, LANES), lambda i: (i, 0)),
                  _mod_spec(5, tps),
                  pl.BlockSpec((1, d), lambda i: (0, 0))],
        out_specs=pl.BlockSpec((tm, d), lambda i: (i, 0)),
        out_shape=jax.ShapeDtypeStruct((n, d), F32),
        scratch_shapes=[pltpu.VMEM((tm, d), F32), pltpu.VMEM((tm, d), F32), pltpu.SemaphoreType.DMA((2,))],
        compiler_params=_params("arbitrary"),
        name="combine",
    )(dest[:, 0].reshape(nb, 1, tm), dest[:, 1].reshape(nb, 1, tm), ys, h, gates, mod, final_g.reshape(1, d))


def _moe(h, mod, g, wr, br, wg, wu, wd, final_g, seq):
    n, d = h.shape
    wr_p = jnp.zeros((d, LANES), BF16).at[:, :N_EXPERTS].set(wr.astype(BF16))
    br_p = jnp.zeros((1, LANES), F32).at[0, :N_EXPERTS].set(br)
    hn, eid, gates, rank, cnt = _router(h, mod, g, wr_p, br_p, seq)
    counts = cnt[0, :N_EXPERTS]
    padded = (counts + MOE_BLOCK - 1) // MOE_BLOCK * MOE_BLOCK
    pad_end = jnp.cumsum(padded)
    pad_start = pad_end - padded
    top_e = eid[:, :TOP_K]
    dest = pad_start[top_e] + rank[:, :TOP_K]
    na = n * TOP_K
    n_blk = -(-na // MOE_BLOCK) + N_EXPERTS
    l = n_blk * MOE_BLOCK
    tok = jnp.broadcast_to(jnp.arange(n, dtype=jnp.int32)[:, None], (n, TOP_K))
    slot_tok = jnp.zeros((l,), jnp.int32).at[dest.reshape(-1)].set(tok.reshape(-1))
    blk_e = jnp.minimum(jnp.searchsorted(pad_end, jnp.arange(n_blk, dtype=jnp.int32) * MOE_BLOCK, side='right'),
                        N_EXPERTS - 1).astype(jnp.int32)
    n_used = (pad_end[-1:] // MOE_BLOCK).astype(jnp.int32)
    xs = _gather_rows(hn, slot_tok)
    ys = _experts(xs, blk_e, n_used, wg, wu, wd)
    return _combine(ys, dest, gates, h, mod, final_g, seq)


def _prep_even_w_in(w):
    q, k, v, u, g = jnp.split(w, [A_Q, A_Q + A_KV, A_Q + 2 * A_KV, A_Q + 2 * A_KV + B_WIDTH], axis=-1)
    return jnp.concatenate([q, u, g, k, v], axis=-1).astype(BF16)


def _prep_odd_w_in(w):
    d = w.shape[0]
    o = 3 * C_WIDTH + D_Q_RANK + D_KV_RANK
    half = D_ROPE // 2
    x1, x2 = w[:, o:o + half], w[:, o + half:o + D_ROPE]
    z64 = jnp.zeros((d, D_NOPE), w.dtype)
    z32 = jnp.zeros((d, D_HEAD_PAD - D_NOPE - D_ROPE), w.dtype)
    return jnp.concatenate([w[:, :o], z64, x1, x2, z32, z64, x2, x1, z32], axis=-1).astype(BF16)


def _prep_w_q_b(w):
    r = w.shape[0]
    half = D_ROPE // 2
    wh = w.reshape(r, D_HEADS, D_NOPE + D_ROPE)
    nope, x1, x2 = wh[..., :D_NOPE], wh[..., D_NOPE:D_NOPE + half], wh[..., D_NOPE + half:]
    z32 = jnp.zeros((r, D_HEADS, D_HEAD_PAD - D_NOPE - D_ROPE), w.dtype)
    wa = jnp.concatenate([nope, x1, x2, z32], axis=-1).reshape(r, D_HEADS * D_HEAD_PAD)
    wb = jnp.concatenate([jnp.zeros_like(nope), x2, x1, z32], axis=-1).reshape(r, D_HEADS * D_HEAD_PAD)
    return wa.astype(BF16), wb.astype(BF16)


def _prep_w_kv_b(w):
    r = w.shape[0]
    wh = w.reshape(r, D_HEADS, D_NOPE + D_VDIM)
    kk = jnp.concatenate([wh[..., :D_NOPE], jnp.zeros((r, D_HEADS, D_HEAD_PAD - D_NOPE), w.dtype)], axis=-1)
    return kk.reshape(r, D_HEADS * D_HEAD_PAD).astype(BF16), wh[..., D_NOPE:].reshape(r, D_HEADS * D_VDIM).astype(BF16)


def _rope_tables(seq):
    half = D_ROPE // 2
    inv = ROPE_THETA ** (-jnp.arange(0, D_ROPE, 2, dtype=F32) / D_ROPE)
    ang = jnp.arange(seq, dtype=F32)[:, None] * inv[None, :]
    cos, sin = jnp.cos(ang), jnp.sin(ang)
    one = jnp.ones((seq, D_NOPE), F32)
    z64 = jnp.zeros((seq, D_NOPE), F32)
    z32 = jnp.zeros((seq, D_HEAD_PAD - D_NOPE - D_ROPE), F32)
    assert half * 2 == D_ROPE
    return (jnp.concatenate([one, cos, cos, z32], axis=-1), jnp.concatenate([z64, -sin, sin, z32], axis=-1))


def kernel(x, c, even_ada_w, even_ada_b, even_norm_mix_g, even_w_in, even_sinks, even_gmlp_ln_g, even_gmlp_ln_b, even_w_s, even_b_s, even_w_o, even_norm_ffn_g, even_ffn_w_gate, even_ffn_w_up, even_ffn_w_down, odd_ada_w, odd_ada_b, odd_norm_mix_g, odd_w_in, odd_conv_w, odd_q_norm_g, odd_w_q_b, odd_kv_norm_g, odd_w_kv_b, odd_w_o, odd_norm_ffn_g, odd_router_w, odd_router_b, odd_exp_w_gate, odd_exp_w_up, odd_exp_w_down, final_norm_g):
    batch, seq, d = x.shape
    assert even_ada_w.shape[0] == 1 and odd_ada_w.shape[0] == 1, "one even and one odd layer"
    h = x.reshape(batch * seq, d)

    mod = _ada(c, even_ada_w[0], even_ada_b[0])
    proj = _even_in(h, mod, even_norm_mix_g[0], _prep_even_w_in(even_w_in[0]), seq)
    a_out, b_out = _even_core(proj, even_sinks[0], even_gmlp_ln_g[0], even_gmlp_ln_b[0], even_w_s[0], even_b_s[0], seq)
    h = _out_proj(a_out, b_out, h, mod, even_w_o[0].astype(BF16), seq)
    h = _ffn(h, mod, even_norm_ffn_g[0], even_ffn_w_gate[0].astype(BF16), even_ffn_w_up[0].astype(BF16),
             even_ffn_w_down[0].astype(BF16), seq)

    mod = _ada(c, odd_ada_w[0], odd_ada_b[0])
    wqa, wqb = _prep_w_q_b(odd_w_q_b[0])
    wkk, wkv = _prep_w_kv_b(odd_w_kv_b[0])
    cos_t, sin_t = _rope_tables(seq)
    c_out, qf, kf, v = _odd_in(h, mod, odd_norm_mix_g[0], _prep_odd_w_in(odd_w_in[0]), odd_conv_w[0],
                               odd_q_norm_g[0], wqa, wqb, odd_kv_norm_g[0], wkk, wkv, cos_t, sin_t, seq)
    d_out = _mla(qf, kf, v, batch, seq)
    h = _out_proj(c_out, d_out, h, mod, odd_w_o[0].astype(BF16), seq)
    out = _moe(h, mod, odd_norm_ffn_g[0], odd_router_w[0], odd_router_b[0], odd_exp_w_gate[0].astype(BF16),
               odd_exp_w_up[0].astype(BF16), odd_exp_w_down[0].astype(BF16), final_norm_g, seq)
    return out.reshape(batch, seq, d)
```

```python
import functools

import jax
import jax.numpy as jnp
import numpy as np
from jax import lax
from jax.experimental import pallas as pl
from jax.experimental.pallas import tpu as pltpu

F32 = jnp.float32
BF16 = jnp.bfloat16

D_MODEL = 1024
NORM_EPS = 1e-6
ADA_CHUNKS = 6
BLOCK = 128

A_HEADS = 8
A_KV_HEADS = 2
A_GROUP = A_HEADS // A_KV_HEADS
A_HEAD_DIM = 64
A_Q = A_HEADS * A_HEAD_DIM
A_KV = A_KV_HEADS * A_HEAD_DIM

B_GROUPS = 8
B_GROUP_DIM = 64
B_WIDTH = B_GROUPS * B_GROUP_DIM
B_CHUNK = 128

C_WIDTH = 512
C_CONV = 3

D_HEADS = 8
D_NOPE = 64
D_ROPE = 32
D_VDIM = 64
D_Q_RANK = 512
D_KV_RANK = 256
ROPE_THETA = 10000.0
D_HEAD_PAD = 128
MLA_Q_SCALE = np.float32((D_NOPE + D_ROPE) ** -0.5 * 1.4426950408889634)

N_EXPERTS = 8
TOP_K = 2
MOE_BLOCK = 512

LANES = 128
NEG_BIG = -1e30
VMEM_LIMIT = 56 * 1024 * 1024


def _params(*sem):
    return pltpu.CompilerParams(dimension_semantics=sem, vmem_limit_bytes=VMEM_LIMIT)


def _dot(a, b):
    return jnp.dot(a, b, preferred_element_type=F32)


def _dot_nt(a, b):
    return lax.dot_general(a, b, (((1,), (1,)), ((), ())), preferred_element_type=F32)


def _modulated_norm(x, g, scale, shift):
    ms = jnp.mean(x * x, axis=-1, keepdims=True)
    return (x * lax.rsqrt(ms + NORM_EPS)) * (g * (1.0 + scale)) + shift


def _gelu(x):
    return 0.5 * x * (1.0 + lax.erf(x * np.float32(0.7071067811865476)))


def _silu(x):
    return x * (1.0 / (1.0 + jnp.exp(-x)))


def _ada_kernel(c_ref, w_ref, b_ref, o_ref):
    cond = _silu(c_ref[...])
    o_ref[...] = _dot(cond.astype(BF16), w_ref[...].astype(BF16)) + b_ref[...]


def _ada(c, w, b):
    bn, d = c.shape
    rows = 8
    tn = 1536
    cp = jnp.zeros((rows, d), F32).at[:bn].set(c)
    out = pl.pallas_call(
        _ada_kernel,
        grid=(w.shape[1] // tn,),
        in_specs=[pl.BlockSpec((rows, d), lambda j: (0, 0)),
                  pl.BlockSpec((d, tn), lambda j: (0, j)),
                  pl.BlockSpec((1, tn), lambda j: (0, j))],
        out_specs=pl.BlockSpec((rows, tn), lambda j: (0, j)),
        out_shape=jax.ShapeDtypeStruct((rows, w.shape[1]), F32),
        compiler_params=_params("arbitrary"),
        name="ada",
    )(cp, w, b.reshape(1, -1))
    return out[:bn].reshape(bn * ADA_CHUNKS, 1, d)


def _mod_spec(chunk, tiles_per_seq):
    return pl.BlockSpec((1, 1, D_MODEL), lambda i, *_: ((i // tiles_per_seq) * ADA_CHUNKS + chunk, 0, 0))


def _even_in_kernel(x_ref, g_ref, sh_ref, sc_ref, w_ref, o_ref):
    hn = _modulated_norm(x_ref[...], g_ref[...], sc_ref[0], sh_ref[0])
    o_ref[...] = _dot(hn.astype(BF16), w_ref[...]).astype(o_ref.dtype)


def _even_in(h, mod, g, w, seq, tm=512):
    n, d = h.shape
    tps = seq // tm
    nout = w.shape[1]
    return pl.pallas_call(
        _even_in_kernel,
        grid=(n // tm,),
        in_specs=[pl.BlockSpec((tm, d), lambda i: (i, 0)),
                  pl.BlockSpec((1, d), lambda i: (0, 0)),
                  _mod_spec(0, tps), _mod_spec(1, tps),
                  pl.BlockSpec((d, nout), lambda i: (0, 0))],
        out_specs=pl.BlockSpec((tm, nout), lambda i: (i, 0)),
        out_shape=jax.ShapeDtypeStruct((n, nout), BF16),
        compiler_params=_params("arbitrary"),
        name="even_in",
    )(h, g.reshape(1, d), mod, mod, w)


EVEN_TQ = 512


def _even_core_kernel(q_ref, u_ref, gg_ref, kv_ref, kvp_ref, sinks_ref, lng_ref, lnb_ref, ws_ref, bst_ref,
                      a_ref, b_ref, *, tiles_per_seq):
    first_key = jnp.where((pl.program_id(0) % tiles_per_seq) == 0, BLOCK, 0)
    nsub = EVEN_TQ // BLOCK

    qi = lax.broadcasted_iota(jnp.int32, (BLOCK, 2 * BLOCK), 0)
    kj = lax.broadcasted_iota(jnp.int32, (BLOCK, 2 * BLOCK), 1)
    dist = qi + BLOCK - kj
    valid = (dist >= 0) & (dist < BLOCK)
    valid_first = valid & (kj >= first_key)
    distf = dist.astype(F32)
    for sb in range(nsub):
        rows = slice(sb * BLOCK, (sb + 1) * BLOCK)
        cur = kv_ref[rows, :]
        prev = kvp_ref[...] if sb == 0 else kv_ref[(sb - 1) * BLOCK:sb * BLOCK, :]
        band = jnp.concatenate([prev, cur], axis=0)
        vmask = valid_first if sb == 0 else valid
        outs = []
        for h in range(A_HEADS):
            kvh = h // A_GROUP
            kh = band[:, kvh * A_HEAD_DIM:(kvh + 1) * A_HEAD_DIM]
            vh = band[:, A_KV + kvh * A_HEAD_DIM:A_KV + (kvh + 1) * A_HEAD_DIM]
            qh = q_ref[rows, h * A_HEAD_DIM:(h + 1) * A_HEAD_DIM]
            s = _dot_nt(qh, kh) * np.float32(A_HEAD_DIM ** -0.5) - np.float32(2.0 ** -(h + 1)) * distf
            s = jnp.where(vmask, s, NEG_BIG)
            sink = sinks_ref[h]
            m = jnp.maximum(jnp.max(s, axis=-1, keepdims=True), sink)
            p = jnp.exp(s - m)
            denom = jnp.sum(p, axis=-1, keepdims=True) + jnp.exp(sink - m)
            outs.append(_dot(p.astype(BF16), vh) / denom)
        a_ref[rows, :] = jnp.concatenate(outs, axis=1).astype(a_ref.dtype)

    gu = _gelu(u_ref[...].astype(F32))
    gv = _gelu(gg_ref[...].astype(F32))
    mu = jnp.mean(gv, axis=-1, keepdims=True)
    cen = gv - mu
    var = jnp.mean(cen * cen, axis=-1, keepdims=True)
    vn = ((cen * lax.rsqrt(var + NORM_EPS)) * lng_ref[...] + lnb_ref[...]).astype(BF16)
    tt = lax.broadcasted_iota(jnp.int32, (B_CHUNK, B_CHUNK), 0)
    ss = lax.broadcasted_iota(jnp.int32, (B_CHUNK, B_CHUNK), 1)
    causal = ss <= tt
    wm = [jnp.where(causal, ws_ref[gp], 0.0).astype(BF16) for gp in range(B_GROUPS)]
    for c in range(EVEN_TQ // B_CHUNK):
        rows = slice(c * B_CHUNK, (c + 1) * B_CHUNK)
        cols = []
        for gp in range(B_GROUPS):
            lanes = slice(gp * B_GROUP_DIM, (gp + 1) * B_GROUP_DIM)
            cols.append(_dot(wm[gp], vn[rows, lanes]) + bst_ref[:, gp:gp + 1])
        b_ref[rows, :] = (gu[rows, :] * jnp.concatenate(cols, axis=1)).astype(b_ref.dtype)


def _even_core(proj, sinks, ln_g, ln_b, w_s, b_s, seq):
    n = proj.shape[0]
    tq = EVEN_TQ
    tps = seq // tq
    sub = tq // BLOCK
    kvb = (A_Q + 2 * B_WIDTH) // (2 * A_KV)
    kern = functools.partial(_even_core_kernel, tiles_per_seq=tps)
    return pl.pallas_call(
        kern,
        grid=(n // tq,),
        in_specs=[pl.BlockSpec((tq, A_Q), lambda i: (i, 0)),
                  pl.BlockSpec((tq, B_WIDTH), lambda i: (i, 1)),
                  pl.BlockSpec((tq, B_WIDTH), lambda i: (i, 2)),
                  pl.BlockSpec((tq, 2 * A_KV), lambda i: (i, kvb)),
                  pl.BlockSpec((BLOCK, 2 * A_KV), lambda i: (jnp.maximum(i * sub - 1, 0), kvb)),
                  pl.BlockSpec(memory_space=pltpu.SMEM),
                  pl.BlockSpec((1, B_WIDTH), lambda i: (0, 0)),
                  pl.BlockSpec((1, B_WIDTH), lambda i: (0, 0)),
                  pl.BlockSpec((B_GROUPS, B_CHUNK, B_CHUNK), lambda i: (0, 0, 0)),
                  pl.BlockSpec((B_CHUNK, B_GROUPS), lambda i: (0, 0))],
        out_specs=[pl.BlockSpec((tq, A_Q), lambda i: (i, 0)),
                   pl.BlockSpec((tq, B_WIDTH), lambda i: (i, 0))],
        out_shape=[jax.ShapeDtypeStruct((n, A_Q), BF16), jax.ShapeDtypeStruct((n, B_WIDTH), BF16)],
        compiler_params=_params("arbitrary"),
        name="even_core",
    )(proj, proj, proj, proj, proj, sinks, ln_g.reshape(1, -1), ln_b.reshape(1, -1), w_s, b_s.T)


def _out_proj_kernel(m1_ref, m2_ref, h_ref, gate_ref, w_ref, o_ref):
    half = m1_ref.shape[1]
    y = _dot(m1_ref[...], w_ref[0:half, :]) + _dot(m2_ref[...], w_ref[half:2 * half, :])
    o_ref[...] = h_ref[...] + gate_ref[0] * y


def _out_proj(m1, m2, h, mod, w, seq, tm=512):
    n, d = h.shape
    tps = seq // tm
    half = m1.shape[1]
    return pl.pallas_call(
        _out_proj_kernel,
        grid=(n // tm,),
        in_specs=[pl.BlockSpec((tm, half), lambda i: (i, 0)),
                  pl.BlockSpec((tm, half), lambda i: (i, 0)),
                  pl.BlockSpec((tm, d), lambda i: (i, 0)),
                  _mod_spec(2, tps),
                  pl.BlockSpec((2 * half, d), lambda i: (0, 0))],
        out_specs=pl.BlockSpec((tm, d), lambda i: (i, 0)),
        out_shape=jax.ShapeDtypeStruct((n, d), F32),
        compiler_params=_params("arbitrary"),
        name="out_proj",
    )(m1, m2, h, mod, w)


def _ffn_kernel(h_ref, g_ref, sh_ref, sc_ref, gate_ref, wg_ref, wu_ref, wd_ref, o_ref, hn_ref, acc_ref):
    c = pl.program_id(1)

    @pl.when(c == 0)
    def _():
        hn_ref[...] = _modulated_norm(h_ref[...], g_ref[...], sc_ref[0], sh_ref[0]).astype(BF16)
        acc_ref[...] = jnp.zeros_like(acc_ref)

    hn = hn_ref[...]
    a = _dot(hn, wg_ref[...])
    b = _dot(hn, wu_ref[...])
    acc_ref[...] += _dot((_silu(a) * b).astype(BF16), wd_ref[...])

    @pl.when(c == pl.num_programs(1) - 1)
    def _():
        o_ref[...] = h_ref[...] + gate_ref[0] * acc_ref[...]


def _ffn(h, mod, g, wg, wu, wd, seq, tm=1024, ch=256):
    n, d = h.shape
    ff = wg.shape[1]
    tps = seq // tm
    return pl.pallas_call(
        _ffn_kernel,
        grid=(n // tm, ff // ch),
        in_specs=[pl.BlockSpec((tm, d), lambda i, c: (i, 0)),
                  pl.BlockSpec((1, d), lambda i, c: (0, 0)),
                  _mod_spec(3, tps), _mod_spec(4, tps), _mod_spec(5, tps),
                  pl.BlockSpec((d, ch), lambda i, c: (0, c)),
                  pl.BlockSpec((d, ch), lambda i, c: (0, c)),
                  pl.BlockSpec((ch, d), lambda i, c: (c, 0))],
        out_specs=pl.BlockSpec((tm, d), lambda i, c: (i, 0)),
        out_shape=jax.ShapeDtypeStruct((n, d), F32),
        scratch_shapes=[pltpu.VMEM((tm, d), BF16), pltpu.VMEM((tm, d), F32)],
        compiler_params=_params("arbitrary", "arbitrary"),
        name="ffn",
    )(h, g.reshape(1, d), mod, mod, mod, wg, wu, wd)


def _odd_in_kernel(x_ref, g_ref, sh_ref, sc_ref, w_ref, cw_ref, qg_ref, wqa_ref, wqb_ref, kg_ref, wkk_ref, wkv_ref,
                   cos_ref, sin_ref, c_ref, q_ref, k_ref, v_ref, carry_ref, *, tiles_per_seq):
    first = (pl.program_id(0) % tiles_per_seq) == 0
    tm = x_ref.shape[0]
    hn = _modulated_norm(x_ref[...], g_ref[...], sc_ref[0], sh_ref[0]).astype(BF16)
    proj = _dot(hn, w_ref[...])
    o = 0
    gb = proj[:, o:o + C_WIDTH]; o += C_WIDTH
    gc = proj[:, o:o + C_WIDTH]; o += C_WIDTH
    xc = proj[:, o:o + C_WIDTH]; o += C_WIDTH
    q_lat = proj[:, o:o + D_Q_RANK]; o += D_Q_RANK
    kv_lat = proj[:, o:o + D_KV_RANK]; o += D_KV_RANK
    kr_a = proj[:, o:o + LANES]; o += LANES
    kr_b = proj[:, o:o + LANES]

    @pl.when(first)
    def _():
        carry_ref[...] = jnp.zeros_like(carry_ref)

    z = gc * xc
    prev1 = carry_ref[7:8, :]
    prev2 = carry_ref[6:7, :]
    row = lax.broadcasted_iota(jnp.int32, z.shape, 0)
    z1 = jnp.where(row == 0, prev1, pltpu.roll(z, 1, 0))
    z2 = jnp.where(row == 0, prev2, jnp.where(row == 1, prev1, pltpu.roll(z, 2, 0)))
    y = cw_ref[0:1, :] * z2 + cw_ref[1:2, :] * z1 + cw_ref[2:3, :] * z
    c_ref[...] = (gb * y).astype(c_ref.dtype)
    carry_ref[...] = z[tm - 8:tm, :]

    cosf = jnp.concatenate([cos_ref[...]] * D_HEADS, axis=1)
    sinf = jnp.concatenate([sin_ref[...]] * D_HEADS, axis=1)
    qn = (q_lat * lax.rsqrt(jnp.mean(q_lat * q_lat, axis=-1, keepdims=True) + NORM_EPS) * qg_ref[...]).astype(BF16)
    q = _dot(qn, wqa_ref[...]) * cosf + _dot(qn, wqb_ref[...]) * sinf
    q_ref[...] = (q * MLA_Q_SCALE).astype(q_ref.dtype)
    kn = (kv_lat * lax.rsqrt(jnp.mean(kv_lat * kv_lat, axis=-1, keepdims=True) + NORM_EPS) * kg_ref[...]).astype(BF16)
    kpe = kr_a * cos_ref[...] + kr_b * sin_ref[...]
    k_ref[...] = (_dot(kn, wkk_ref[...]) + jnp.concatenate([kpe] * D_HEADS, axis=1)).astype(k_ref.dtype)
    v_ref[...] = _dot(kn, wkv_ref[...]).astype(v_ref.dtype)


def _odd_in(h, mod, g, w, conv_w, qg, wqa, wqb, kg, wkk, wkv, cos_t, sin_t, seq, tm=512):
    n, d = h.shape
    tps = seq // tm
    hw = D_HEADS * D_HEAD_PAD
    const = lambda a: pl.BlockSpec(a.shape, lambda i: (0,) * a.ndim)
    qg = qg.reshape(1, -1)
    kg = kg.reshape(1, -1)
    kern = functools.partial(_odd_in_kernel, tiles_per_seq=tps)
    return pl.pallas_call(
        kern,
        grid=(n // tm,),
        in_specs=[pl.BlockSpec((tm, d), lambda i: (i, 0)),
                  pl.BlockSpec((1, d), lambda i: (0, 0)),
                  _mod_spec(0, tps), _mod_spec(1, tps),
                  const(w), const(conv_w), const(qg), const(wqa), const(wqb), const(kg), const(wkk), const(wkv),
                  pl.BlockSpec((tm, LANES), lambda i: (i % tps, 0)),
                  pl.BlockSpec((tm, LANES), lambda i: (i % tps, 0))],
        out_specs=[pl.BlockSpec((tm, C_WIDTH), lambda i: (i, 0)),
                   pl.BlockSpec((tm, hw), lambda i: (i, 0)),
                   pl.BlockSpec((tm, hw), lambda i: (i, 0)),
                   pl.BlockSpec((tm, D_HEADS * D_VDIM), lambda i: (i, 0))],
        out_shape=[jax.ShapeDtypeStruct((n, C_WIDTH), BF16),
                   jax.ShapeDtypeStruct((n, hw), BF16),
                   jax.ShapeDtypeStruct((n, hw), BF16),
                   jax.ShapeDtypeStruct((n, D_HEADS * D_VDIM), BF16)],
        scratch_shapes=[pltpu.VMEM((8, C_WIDTH), F32)],
        compiler_params=_params("arbitrary"),
        name="odd_in",
    )(h, g.reshape(1, d), mod, mod, w, conv_w, qg, wqa, wqb, kg, wkk, wkv, cos_t, sin_t)


MLA_T = 512


def _mla_kernel(qi_ref, kj_ref, q_ref, k_ref, v_ref, o_ref, m_ref, l_ref, acc_ref):
    t = pl.program_id(1)
    qi = qi_ref[t]
    kj = kj_ref[t]
    tq, tk = MLA_T, MLA_T
    rep = tk // LANES

    @pl.when(kj == 0)
    def _():
        m_ref[...] = jnp.full_like(m_ref, NEG_BIG)
        l_ref[...] = jnp.zeros_like(l_ref)
        acc_ref[...] = jnp.zeros_like(acc_ref)

    def tile(masked):
        if masked:
            row = lax.broadcasted_iota(jnp.int32, (tq, tk), 0)
            col = lax.broadcasted_iota(jnp.int32, (tq, tk), 1)
            keep = col <= row
        for hp in range(D_HEADS // 2):
            ps, alphas = [], []
            for h in (2 * hp, 2 * hp + 1):
                hl = slice(h * D_HEAD_PAD, (h + 1) * D_HEAD_PAD)
                s = _dot_nt(q_ref[:, hl], k_ref[:, hl])
                if masked:
                    s = jnp.where(keep, s, NEG_BIG)
                m_prev = m_ref[h]
                m_next = jnp.maximum(m_prev, jnp.max(s, axis=-1, keepdims=True))
                alpha = jnp.exp2(m_prev - m_next)
                p = jnp.exp2(s - jnp.concatenate([m_next] * rep, axis=1))
                l_ref[h] = alpha * l_ref[h] + jnp.sum(p, axis=-1, keepdims=True)
                m_ref[h] = m_next
                ps.append(p.astype(BF16))
                alphas.append(alpha)
            pv = _dot(jnp.concatenate(ps, axis=0), v_ref[:, hp * LANES:(hp + 1) * LANES])
            acc_ref[hp] = jnp.concatenate(alphas, axis=0) * acc_ref[hp] + pv

    @pl.when(kj < qi)
    def _():
        tile(False)

    @pl.when(kj == qi)
    def _():
        tile(True)
        lane = lax.broadcasted_iota(jnp.int32, (tq, LANES), 1)
        for hp in range(D_HEADS // 2):
            lo = acc_ref[hp, 0:tq, :] / l_ref[2 * hp]
            hi = acc_ref[hp, tq:2 * tq, :] / l_ref[2 * hp + 1]
            o_ref[:, hp * LANES:(hp + 1) * LANES] = jnp.where(lane < D_VDIM, lo, hi).astype(o_ref.dtype)


def _mla(qf, kf, v, batch, seq):
    n = qf.shape[0]
    t = MLA_T
    nq = seq // t
    pairs = [(i, j) for i in range(nq) for j in range(i + 1)]
    qi = jnp.asarray([p[0] for p in pairs], jnp.int32)
    kj = jnp.asarray([p[1] for p in pairs], jnp.int32)
    hw = D_HEADS * D_HEAD_PAD
    vw = D_HEADS * D_VDIM
    grid_spec = pltpu.PrefetchScalarGridSpec(
        num_scalar_prefetch=2,
        grid=(batch, len(pairs)),
        in_specs=[pl.BlockSpec((t, hw), lambda b, s, qi, kj: (b * nq + qi[s], 0)),
                  pl.BlockSpec((t, hw), lambda b, s, qi, kj: (b * nq + kj[s], 0)),
                  pl.BlockSpec((t, vw), lambda b, s, qi, kj: (b * nq + kj[s], 0))],
        out_specs=pl.BlockSpec((t, vw), lambda b, s, qi, kj: (b * nq + qi[s], 0)),
        scratch_shapes=[pltpu.VMEM((D_HEADS, t, LANES), F32),
                        pltpu.VMEM((D_HEADS, t, LANES), F32),
                        pltpu.VMEM((D_HEADS // 2, 2 * t, LANES), F32)],
    )
    return pl.pallas_call(
        _mla_kernel,
        grid_spec=grid_spec,
        out_shape=jax.ShapeDtypeStruct((n, vw), BF16),
        compiler_params=_params("arbitrary", "arbitrary"),
        name="mla",
    )(qi, kj, qf, kf, v)


ROUTER_TM = 512
ROW_TILE = D_MODEL // LANES


def _store_row_tiles(ref, x):
    rows = x.shape[0]
    for s in range(ROW_TILE):
        ref[pl.ds(s, rows, stride=ROW_TILE), :] = x[:, s * LANES:(s + 1) * LANES]


def _load_row_tiles(ref, rows):
    return jnp.concatenate([ref[pl.ds(s, rows, stride=ROW_TILE), :] for s in range(ROW_TILE)], axis=1)


def _tile_copy(src_ref, dst_ref, src_row, dst_row, sem):
    return pltpu.make_async_copy(src_ref.at[pl.ds(pl.multiple_of(src_row * ROW_TILE, ROW_TILE), ROW_TILE)],
                                 dst_ref.at[pl.ds(pl.multiple_of(dst_row * ROW_TILE, ROW_TILE), ROW_TILE)], sem)


def _router_kernel(h_ref, g_ref, sh_ref, sc_ref, wr_ref, br_ref, hn_ref, eid_ref, gate_ref, rank_ref, cnt_ref,
                   carry_ref):
    tm = h_ref.shape[0]

    @pl.when(pl.program_id(0) == 0)
    def _():
        carry_ref[...] = jnp.zeros_like(carry_ref)

    hn32 = _modulated_norm(h_ref[...], g_ref[...], sc_ref[0], sh_ref[0])
    _store_row_tiles(hn_ref, hn32)
    hn = hn32.astype(BF16)
    lane = lax.broadcasted_iota(jnp.int32, (tm, LANES), 1)
    logits = jnp.where(lane < N_EXPERTS, _dot(hn, wr_ref[...]) + br_ref[...], NEG_BIG)
    m1 = jnp.max(logits, axis=-1, keepdims=True)
    e1 = jnp.min(jnp.where(logits == m1, lane, LANES), axis=-1, keepdims=True)
    rest = jnp.where(lane == e1, NEG_BIG, logits)
    m2 = jnp.max(rest, axis=-1, keepdims=True)
    e2 = jnp.min(jnp.where(rest == m2, lane, LANES), axis=-1, keepdims=True)
    ex = jnp.exp(m2 - m1)
    g1 = 1.0 / (1.0 + ex)
    g2 = ex / (1.0 + ex)
    eid_ref[...] = jnp.where(lane == 0, e1, jnp.where(lane == 1, e2, 0))
    gate_ref[...] = jnp.where(lane == 0, g1, jnp.where(lane == 1, g2, 0.0))
    onehot = ((lane == e1) | (lane == e2)).astype(BF16)
    tt = lax.broadcasted_iota(jnp.int32, (tm, tm), 0)
    ss = lax.broadcasted_iota(jnp.int32, (tm, tm), 1)
    before = (ss < tt).astype(BF16)
    prefix = _dot(before, onehot) + carry_ref[...]
    r1 = jnp.sum(jnp.where(lane == e1, prefix, 0.0), axis=-1, keepdims=True)
    r2 = jnp.sum(jnp.where(lane == e2, prefix, 0.0), axis=-1, keepdims=True)
    rank_ref[...] = jnp.where(lane == 0, r1, jnp.where(lane == 1, r2, 0.0)).astype(jnp.int32)
    total = carry_ref[...] + jnp.sum(onehot.astype(F32), axis=0, keepdims=True)
    carry_ref[...] = total
    cnt_ref[...] = jnp.broadcast_to(total, cnt_ref.shape).astype(jnp.int32)


def _router(h, mod, g, wr, br, seq):
    n, d = h.shape
    tm = ROUTER_TM
    tps = seq // tm
    wide = lambda dt: jax.ShapeDtypeStruct((n, LANES), dt)
    row_spec = pl.BlockSpec((tm, LANES), lambda i: (i, 0))
    return pl.pallas_call(
        _router_kernel,
        grid=(n // tm,),
        in_specs=[pl.BlockSpec((tm, d), lambda i: (i, 0)),
                  pl.BlockSpec((1, d), lambda i: (0, 0)),
                  _mod_spec(3, tps), _mod_spec(4, tps),
                  pl.BlockSpec((d, LANES), lambda i: (0, 0)),
                  pl.BlockSpec((1, LANES), lambda i: (0, 0))],
        out_specs=[pl.BlockSpec((tm * ROW_TILE, LANES), lambda i: (i, 0)), row_spec, row_spec, row_spec,
                   pl.BlockSpec((8, LANES), lambda i: (0, 0))],
        out_shape=[jax.ShapeDtypeStruct((n * ROW_TILE, LANES), F32), wide(jnp.int32), wide(F32), wide(jnp.int32),
                   jax.ShapeDtypeStruct((8, LANES), jnp.int32)],
        scratch_shapes=[pltpu.VMEM((1, LANES), F32)],
        compiler_params=_params("arbitrary"),
        name="router",
    )(h, g.reshape(1, d), mod, mod, wr, br)


DISPATCH_TM = 512
ISSUE_UNROLL = 8


def _dispatch_kernel(d0_ref, d1_ref, zrow_ref, hn_ref, xs_ref, zero_ref, sem, zsem, *, tail_start):
    tm = DISPATCH_TM

    @pl.when(pl.program_id(0) == 0)
    def _():
        zero_ref[...] = jnp.zeros_like(zero_ref)
        blk = MOE_BLOCK * ROW_TILE
        tail = range(tail_start // MOE_BLOCK, xs_ref.shape[0] // blk)
        for start in [pl.multiple_of(zrow_ref[e] * ROW_TILE, ROW_TILE) for e in range(N_EXPERTS)] + \
                     [t * blk for t in tail]:
            clear = pltpu.make_async_copy(zero_ref, xs_ref.at[pl.ds(start, blk)], zsem)
            clear.start()
            clear.wait()

    def issue(r, carry):
        _tile_copy(hn_ref, xs_ref, r, d0_ref[0, 0, r], sem.at[0]).start()
        _tile_copy(hn_ref, xs_ref, r, d1_ref[0, 0, r], sem.at[1]).start()
        return carry

    lax.fori_loop(0, tm, issue, 0, unroll=ISSUE_UNROLL)
    pltpu.make_async_copy(hn_ref, xs_ref.at[pl.ds(0, tm * ROW_TILE)], sem.at[0]).wait()
    pltpu.make_async_copy(hn_ref, xs_ref.at[pl.ds(0, tm * ROW_TILE)], sem.at[1]).wait()


def _dispatch(hn_tiles, dest, zero_row, n_slots):
    tm = DISPATCH_TM
    n = dest.shape[0]
    nb = n // tm
    idx_spec = pl.BlockSpec((1, 1, tm), lambda i: (i, 0, 0), memory_space=pltpu.SMEM)
    return pl.pallas_call(
        functools.partial(_dispatch_kernel, tail_start=n * TOP_K),
        grid=(nb,),
        in_specs=[idx_spec, idx_spec,
                  pl.BlockSpec(memory_space=pltpu.SMEM),
                  pl.BlockSpec((tm * ROW_TILE, LANES), lambda i: (i, 0))],
        out_specs=pl.BlockSpec(memory_space=pl.ANY),
        out_shape=jax.ShapeDtypeStruct((n_slots * ROW_TILE, LANES), F32),
        scratch_shapes=[pltpu.VMEM((MOE_BLOCK * ROW_TILE, LANES), F32), pltpu.SemaphoreType.DMA((2,)),
                        pltpu.SemaphoreType.DMA(())],
        compiler_params=_params("arbitrary"),
        name="dispatch",
    )(dest[:, 0].reshape(nb, 1, tm), dest[:, 1].reshape(nb, 1, tm), zero_row, hn_tiles)


def _expert_kernel(be_ref, nu_ref, x_ref, wg_ref, wu_ref, wd_ref, o_ref, xb_ref, acc_ref):
    b = pl.program_id(0)
    c = pl.program_id(1)
    used = b < nu_ref[0]
    bm = MOE_BLOCK

    @pl.when(c == 0)
    def _():
        acc_ref[...] = jnp.zeros_like(acc_ref)
        xb_ref[...] = _load_row_tiles(x_ref, bm).astype(BF16)

    @pl.when(used)
    def _():
        x = xb_ref[...]
        a = _dot(x, wg_ref[0])
        u = _dot(x, wu_ref[0])
        acc_ref[...] += _dot((_silu(a) * u).astype(BF16), wd_ref[0])

    @pl.when(c == pl.num_programs(1) - 1)
    def _():
        _store_row_tiles(o_ref, acc_ref[...])


def _experts(xs, blk_e, n_used, n_blk, wg, wu, wd, ch=512):
    d = D_MODEL
    ff = wg.shape[2]
    bm = MOE_BLOCK
    last = lambda b, nu: jnp.minimum(b, nu[0] - 1)
    grid_spec = pltpu.PrefetchScalarGridSpec(
        num_scalar_prefetch=2,
        grid=(n_blk, ff // ch),
        in_specs=[pl.BlockSpec((bm * ROW_TILE, LANES), lambda b, c, be, nu: (last(b, nu), 0)),
                  pl.BlockSpec((1, d, ch), lambda b, c, be, nu: (be[b], 0, c)),
                  pl.BlockSpec((1, d, ch), lambda b, c, be, nu: (be[b], 0, c)),
                  pl.BlockSpec((1, ch, d), lambda b, c, be, nu: (be[b], c, 0))],
        out_specs=pl.BlockSpec((bm * ROW_TILE, LANES), lambda b, c, be, nu: (b, 0)),
        scratch_shapes=[pltpu.VMEM((bm, d), BF16), pltpu.VMEM((bm, d), F32)],
    )
    return pl.pallas_call(
        _expert_kernel,
        grid_spec=grid_spec,
        out_shape=jax.ShapeDtypeStruct((n_blk * bm * ROW_TILE, LANES), F32),
        compiler_params=_params("arbitrary", "arbitrary"),
        name="experts",
    )(blk_e, n_used, xs, wg, wu, wd)


COMBINE_TM = 256


def _combine_kernel(d0_ref, d1_ref, ys_ref, h_ref, gates_ref, gate_ref, fg_ref, o_ref, buf0, buf1, sem):
    tm = COMBINE_TM

    def issue(r, carry):
        _tile_copy(ys_ref, buf0, d0_ref[0, 0, r], r, sem.at[0]).start()
        _tile_copy(ys_ref, buf1, d1_ref[0, 0, r], r, sem.at[1]).start()
        return carry

    lax.fori_loop(0, tm, issue, 0, unroll=ISSUE_UNROLL)
    pltpu.make_async_copy(ys_ref.at[pl.ds(0, tm * ROW_TILE)], buf0, sem.at[0]).wait()
    pltpu.make_async_copy(ys_ref.at[pl.ds(0, tm * ROW_TILE)], buf1, sem.at[1]).wait()
    moe = gates_ref[:, 0:1] * _load_row_tiles(buf0, tm) + gates_ref[:, 1:2] * _load_row_tiles(buf1, tm)
    hh = h_ref[...] + gate_ref[0] * moe
    r = lax.rsqrt(jnp.mean(hh * hh, axis=-1, keepdims=True) + NORM_EPS)
    o_ref[...] = (hh * r) * fg_ref[...]


def _combine(ys, dest, gates, h, mod, final_g, seq):
    n, d = h.shape
    tm = COMBINE_TM
    tps = seq // tm
    nb = n // tm
    idx_spec = pl.BlockSpec((1, 1, tm), lambda i: (i, 0, 0), memory_space=pltpu.SMEM)
    return pl.pallas_call(
        _combine_kernel,
        grid=(nb,),
        in_specs=[idx_spec, idx_spec,
                  pl.BlockSpec(memory_space=pl.ANY),
                  pl.BlockSpec((tm, d), lambda i: (i, 0)),
                  pl.BlockSpec((tm, LANES), lambda i: (i, 0)),
                  _mod_spec(5, tps),
                  pl.BlockSpec((1, d), lambda i: (0, 0))],
        out_specs=pl.BlockSpec((tm, d), lambda i: (i, 0)),
        out_shape=jax.ShapeDtypeStruct((n, d), F32),
        scratch_shapes=[pltpu.VMEM((tm * ROW_TILE, LANES), F32), pltpu.VMEM((tm * ROW_TILE, LANES), F32),
                        pltpu.SemaphoreType.DMA((2,))],
        compiler_params=_params("arbitrary"),
        name="combine",
    )(dest[:, 0].reshape(nb, 1, tm), dest[:, 1].reshape(nb, 1, tm), ys, h, gates, mod, final_g.reshape(1, d))


def _moe(h, mod, g, wr, br, wg, wu, wd, final_g, seq):
    n, d = h.shape
    wr_p = jnp.zeros((d, LANES), BF16).at[:, :N_EXPERTS].set(wr.astype(BF16))
    br_p = jnp.zeros((1, LANES), F32).at[0, :N_EXPERTS].set(br)
    hn, eid, gates, rank, cnt = _router(h, mod, g, wr_p, br_p, seq)
    counts = cnt[0, :N_EXPERTS]
    padded = (counts + MOE_BLOCK - 1) // MOE_BLOCK * MOE_BLOCK
    pad_end = jnp.cumsum(padded)
    pad_start = pad_end - padded
    top_e = eid[:, :TOP_K]
    dest = pad_start[top_e] + rank[:, :TOP_K]
    na = n * TOP_K
    n_blk = -(-na // MOE_BLOCK) + N_EXPERTS
    blk_e = jnp.minimum(jnp.searchsorted(pad_end, jnp.arange(n_blk, dtype=jnp.int32) * MOE_BLOCK, side='right'),
                        N_EXPERTS - 1).astype(jnp.int32)
    n_used = (pad_end[-1:] // MOE_BLOCK).astype(jnp.int32)
    xs = _dispatch(hn, dest, (pad_start + counts).astype(jnp.int32), (n_blk + 1) * MOE_BLOCK)
    ys = _experts(xs, blk_e, n_used, n_blk, wg, wu, wd)
    return _combine(ys, dest, gates, h, mod, final_g, seq)


def _prep_even_w_in(w):
    q, k, v, u, g = jnp.split(w, [A_Q, A_Q + A_KV, A_Q + 2 * A_KV, A_Q + 2 * A_KV + B_WIDTH], axis=-1)
    return jnp.concatenate([q, u, g, k, v], axis=-1).astype(BF16)


def _prep_odd_w_in(w):
    d = w.shape[0]
    o = 3 * C_WIDTH + D_Q_RANK + D_KV_RANK
    half = D_ROPE // 2
    x1, x2 = w[:, o:o + half], w[:, o + half:o + D_ROPE]
    z64 = jnp.zeros((d, D_NOPE), w.dtype)
    z32 = jnp.zeros((d, D_HEAD_PAD - D_NOPE - D_ROPE), w.dtype)
    return jnp.concatenate([w[:, :o], z64, x1, x2, z32, z64, x2, x1, z32], axis=-1).astype(BF16)


def _prep_w_q_b(w):
    r = w.shape[0]
    half = D_ROPE // 2
    wh = w.reshape(r, D_HEADS, D_NOPE + D_ROPE)
    nope, x1, x2 = wh[..., :D_NOPE], wh[..., D_NOPE:D_NOPE + half], wh[..., D_NOPE + half:]
    z32 = jnp.zeros((r, D_HEADS, D_HEAD_PAD - D_NOPE - D_ROPE), w.dtype)
    wa = jnp.concatenate([nope, x1, x2, z32], axis=-1).reshape(r, D_HEADS * D_HEAD_PAD)
    wb = jnp.concatenate([jnp.zeros_like(nope), x2, x1, z32], axis=-1).reshape(r, D_HEADS * D_HEAD_PAD)
    return wa.astype(BF16), wb.astype(BF16)


def _prep_w_kv_b(w):
    r = w.shape[0]
    wh = w.reshape(r, D_HEADS, D_NOPE + D_VDIM)
    kk = jnp.concatenate([wh[..., :D_NOPE], jnp.zeros((r, D_HEADS, D_HEAD_PAD - D_NOPE), w.dtype)], axis=-1)
    return kk.reshape(r, D_HEADS * D_HEAD_PAD).astype(BF16), wh[..., D_NOPE:].reshape(r, D_HEADS * D_VDIM).astype(BF16)


def _rope_tables(seq):
    half = D_ROPE // 2
    inv = ROPE_THETA ** (-jnp.arange(0, D_ROPE, 2, dtype=F32) / D_ROPE)
    ang = jnp.arange(seq, dtype=F32)[:, None] * inv[None, :]
    cos, sin = jnp.cos(ang), jnp.sin(ang)
    one = jnp.ones((seq, D_NOPE), F32)
    z64 = jnp.zeros((seq, D_NOPE), F32)
    z32 = jnp.zeros((seq, D_HEAD_PAD - D_NOPE - D_ROPE), F32)
    assert half * 2 == D_ROPE
    return (jnp.concatenate([one, cos, cos, z32], axis=-1), jnp.concatenate([z64, -sin, sin, z32], axis=-1))


def kernel(x, c, even_ada_w, even_ada_b, even_norm_mix_g, even_w_in, even_sinks, even_gmlp_ln_g, even_gmlp_ln_b, even_w_s, even_b_s, even_w_o, even_norm_ffn_g, even_ffn_w_gate, even_ffn_w_up, even_ffn_w_down, odd_ada_w, odd_ada_b, odd_norm_mix_g, odd_w_in, odd_conv_w, odd_q_norm_g, odd_w_q_b, odd_kv_norm_g, odd_w_kv_b, odd_w_o, odd_norm_ffn_g, odd_router_w, odd_router_b, odd_exp_w_gate, odd_exp_w_up, odd_exp_w_down, final_norm_g):
    batch, seq, d = x.shape
    assert even_ada_w.shape[0] == 1 and odd_ada_w.shape[0] == 1, "one even and one odd layer"
    h = x.reshape(batch * seq, d)

    mod = _ada(c, even_ada_w[0], even_ada_b[0])
    proj = _even_in(h, mod, even_norm_mix_g[0], _prep_even_w_in(even_w_in[0]), seq)
    a_out, b_out = _even_core(proj, even_sinks[0], even_gmlp_ln_g[0], even_gmlp_ln_b[0], even_w_s[0], even_b_s[0], seq)
    h = _out_proj(a_out, b_out, h, mod, even_w_o[0].astype(BF16), seq)
    h = _ffn(h, mod, even_norm_ffn_g[0], even_ffn_w_gate[0].astype(BF16), even_ffn_w_up[0].astype(BF16),
             even_ffn_w_down[0].astype(BF16), seq)

    mod = _ada(c, odd_ada_w[0], odd_ada_b[0])
    wqa, wqb = _prep_w_q_b(odd_w_q_b[0])
    wkk, wkv = _prep_w_kv_b(odd_w_kv_b[0])
    cos_t, sin_t = _rope_tables(seq)
    c_out, qf, kf, v = _odd_in(h, mod, odd_norm_mix_g[0], _prep_odd_w_in(odd_w_in[0]), odd_conv_w[0],
                               odd_q_norm_g[0], wqa, wqb, odd_kv_norm_g[0], wkk, wkv, cos_t, sin_t, seq)
    d_out = _mla(qf, kf, v, batch, seq)
    h = _out_proj(c_out, d_out, h, mod, odd_w_o[0].astype(BF16), seq)
    out = _moe(h, mod, odd_norm_ffn_g[0], odd_router_w[0], odd_router_b[0], odd_exp_w_gate[0].astype(BF16),
               odd_exp_w_up[0].astype(BF16), odd_exp_w_down[0].astype(BF16), final_norm_g, seq)
    return out.reshape(batch, seq, d)
```

```python
import functools

import jax
import jax.numpy as jnp
import numpy as np
from jax import lax
from jax.experimental import pallas as pl
from jax.experimental.pallas import tpu as pltpu

F32 = jnp.float32
BF16 = jnp.bfloat16

D_MODEL = 1024
NORM_EPS = 1e-6
ADA_CHUNKS = 6
BLOCK = 128

A_HEADS = 8
A_KV_HEADS = 2
A_GROUP = A_HEADS // A_KV_HEADS
A_HEAD_DIM = 64
A_Q = A_HEADS * A_HEAD_DIM
A_KV = A_KV_HEADS * A_HEAD_DIM

B_GROUPS = 8
B_GROUP_DIM = 64
B_WIDTH = B_GROUPS * B_GROUP_DIM
B_CHUNK = 128

C_WIDTH = 512
C_CONV = 3

D_HEADS = 8
D_NOPE = 64
D_ROPE = 32
D_VDIM = 64
D_Q_RANK = 512
D_KV_RANK = 256
ROPE_THETA = 10000.0
D_HEAD_PAD = 128
LOG2E = 1.4426950408889634
MLA_Q_SCALE = np.float32((D_NOPE + D_ROPE) ** -0.5 * LOG2E)
SWA_Q_SCALE = np.float32(A_HEAD_DIM ** -0.5 * LOG2E)

N_EXPERTS = 8
TOP_K = 2
MOE_BLOCK = 512

LANES = 128
NEG_BIG = -1e30
VMEM_LIMIT = 56 * 1024 * 1024


def _params(*sem):
    return pltpu.CompilerParams(dimension_semantics=sem, vmem_limit_bytes=VMEM_LIMIT)


def _dot(a, b):
    return jnp.dot(a, b, preferred_element_type=F32)


def _dot_nt(a, b):
    return lax.dot_general(a, b, (((1,), (1,)), ((), ())), preferred_element_type=F32)


def _modulated_norm(x, g, scale, shift):
    ms = jnp.mean(x * x, axis=-1, keepdims=True)
    return (x * lax.rsqrt(ms + NORM_EPS)) * (g * (1.0 + scale)) + shift


def _gelu(x):
    return 0.5 * x * (1.0 + lax.erf(x * np.float32(0.7071067811865476)))


def _silu(x):
    return x * (1.0 / (1.0 + jnp.exp(-x)))


def _ada_kernel(c_ref, w_ref, b_ref, o_ref):
    cond = _silu(c_ref[...])
    o_ref[...] = _dot(cond.astype(BF16), w_ref[...].astype(BF16)) + b_ref[...]


def _ada(c, w, b):
    bn, d = c.shape
    rows = 8
    tn = 1536
    cp = jnp.zeros((rows, d), F32).at[:bn].set(c)
    out = pl.pallas_call(
        _ada_kernel,
        grid=(w.shape[1] // tn,),
        in_specs=[pl.BlockSpec((rows, d), lambda j: (0, 0)),
                  pl.BlockSpec((d, tn), lambda j: (0, j)),
                  pl.BlockSpec((1, tn), lambda j: (0, j))],
        out_specs=pl.BlockSpec((rows, tn), lambda j: (0, j)),
        out_shape=jax.ShapeDtypeStruct((rows, w.shape[1]), F32),
        compiler_params=_params("arbitrary"),
        name="ada",
    )(cp, w, b.reshape(1, -1))
    return out[:bn].reshape(bn * ADA_CHUNKS, 1, d)


def _mod_spec(chunk, tiles_per_seq):
    return pl.BlockSpec((1, 1, D_MODEL), lambda i, *_: ((i // tiles_per_seq) * ADA_CHUNKS + chunk, 0, 0))


def _even_in_kernel(x_ref, g_ref, sh_ref, sc_ref, w_ref, o_ref):
    hn = _modulated_norm(x_ref[...], g_ref[...], sc_ref[0], sh_ref[0])
    proj = _dot(hn.astype(BF16), w_ref[...])
    o_ref[:, 0:A_Q] = (proj[:, 0:A_Q] * SWA_Q_SCALE).astype(o_ref.dtype)
    o_ref[:, A_Q:] = proj[:, A_Q:].astype(o_ref.dtype)


def _even_in(h, mod, g, w, seq, tm=512):
    n, d = h.shape
    tps = seq // tm
    nout = w.shape[1]
    return pl.pallas_call(
        _even_in_kernel,
        grid=(n // tm,),
        in_specs=[pl.BlockSpec((tm, d), lambda i: (i, 0)),
                  pl.BlockSpec((1, d), lambda i: (0, 0)),
                  _mod_spec(0, tps), _mod_spec(1, tps),
                  pl.BlockSpec((d, nout), lambda i: (0, 0))],
        out_specs=pl.BlockSpec((tm, nout), lambda i: (i, 0)),
        out_shape=jax.ShapeDtypeStruct((n, nout), BF16),
        compiler_params=_params("arbitrary"),
        name="even_in",
    )(h, g.reshape(1, d), mod, mod, w)


EVEN_TQ = 512


def _even_core_kernel(q_ref, u_ref, gg_ref, k_ref, kp_ref, v_ref, vp_ref, sinks_ref, bias_ref, lng_ref, lnb_ref,
                      ws_ref, bsp_ref, a_ref, b_ref, *, tiles_per_seq):
    lane = lax.broadcasted_iota(jnp.int32, (BLOCK, LANES), 1)
    low = lane < A_HEAD_DIM
    first_key = jnp.where((pl.program_id(0) % tiles_per_seq) == 0, BLOCK, 0)
    kj = lax.broadcasted_iota(jnp.int32, (A_GROUP * BLOCK, 2 * BLOCK), 1)
    nsub = EVEN_TQ // BLOCK

    for sb in range(nsub):
        rows = slice(sb * BLOCK, (sb + 1) * BLOCK)
        prow = slice((sb - 1) * BLOCK, sb * BLOCK)
        for kvh in range(A_KV_HEADS):
            kl = slice(kvh * LANES, (kvh + 1) * LANES)
            kband = jnp.concatenate([kp_ref[:, kl] if sb == 0 else k_ref[prow, kl], k_ref[rows, kl]], axis=0)
            vband = jnp.concatenate([vp_ref[:, kl] if sb == 0 else v_ref[prow, kl], v_ref[rows, kl]], axis=0)
            stack = []
            for j in (2 * kvh, 2 * kvh + 1):
                qb = q_ref[rows, j * LANES:(j + 1) * LANES]
                stack += [jnp.where(low, qb, jnp.zeros_like(qb)), jnp.where(low, jnp.zeros_like(qb), qb)]
            s = _dot_nt(jnp.concatenate(stack, axis=0), kband) + bias_ref[kvh]
            if sb == 0:
                s = jnp.where(kj >= first_key, s, NEG_BIG)
            sink = jnp.concatenate(
                [jnp.full((BLOCK, 1), sinks_ref[kvh * A_GROUP + g] * np.float32(LOG2E)) for g in range(A_GROUP)],
                axis=0)
            m = jnp.maximum(jnp.max(s, axis=-1, keepdims=True), sink)
            p = jnp.exp2(s - m)
            denom = jnp.sum(p, axis=-1, keepdims=True) + jnp.exp2(sink - m)
            o = _dot(p.astype(BF16), vband) / denom
            for t, j in enumerate((2 * kvh, 2 * kvh + 1)):
                pair = jnp.where(low, o[2 * t * BLOCK:(2 * t + 1) * BLOCK], o[(2 * t + 1) * BLOCK:(2 * t + 2) * BLOCK])
                a_ref[rows, j * LANES:(j + 1) * LANES] = pair.astype(a_ref.dtype)

    gu = _gelu(u_ref[...].astype(F32))
    gv = _gelu(gg_ref[...].astype(F32))
    mu = jnp.mean(gv, axis=-1, keepdims=True)
    cen = gv - mu
    var = jnp.mean(cen * cen, axis=-1, keepdims=True)
    vn = ((cen * lax.rsqrt(var + NORM_EPS)) * lng_ref[...] + lnb_ref[...]).astype(BF16)
    tt = lax.broadcasted_iota(jnp.int32, (B_CHUNK, B_CHUNK), 0)
    ss = lax.broadcasted_iota(jnp.int32, (B_CHUNK, B_CHUNK), 1)
    causal = ss <= tt
    for jp in range(B_GROUPS // 2):
        bl = slice(jp * LANES, (jp + 1) * LANES)
        wcat = jnp.concatenate([jnp.where(causal, ws_ref[2 * jp], 0.0), jnp.where(causal, ws_ref[2 * jp + 1], 0.0)],
                               axis=1).astype(BF16)
        for c in range(EVEN_TQ // B_CHUNK):
            rows = slice(c * B_CHUNK, (c + 1) * B_CHUNK)
            vb = vn[rows, bl]
            rhs = jnp.concatenate([jnp.where(low, vb, jnp.zeros_like(vb)), jnp.where(low, jnp.zeros_like(vb), vb)],
                                  axis=0)
            b_ref[rows, bl] = (gu[rows, bl] * (_dot(wcat, rhs) + bsp_ref[jp])).astype(b_ref.dtype)


def _swa_bias():
    qi = np.arange(BLOCK)[:, None]
    kj = np.arange(2 * BLOCK)[None, :]
    dist = qi + BLOCK - kj
    valid = (dist >= 0) & (dist < BLOCK)
    slopes = 2.0 ** (-8.0 * np.arange(1, A_HEADS + 1) / A_HEADS)
    per_head = np.where(valid[None], -slopes[:, None, None] * dist[None] * LOG2E, NEG_BIG)
    return jnp.asarray(per_head.reshape(A_KV_HEADS, A_GROUP * BLOCK, 2 * BLOCK), F32)


def _even_core(proj, sinks, ln_g, ln_b, w_s, b_s, seq):
    n = proj.shape[0]
    tq = EVEN_TQ
    tps = seq // tq
    sub = tq // BLOCK
    dup = 2 * A_KV
    kb = (A_Q + 2 * B_WIDTH) // dup
    prev = lambda i: jnp.maximum(i * sub - 1, 0)
    bsp = jnp.repeat(b_s.reshape(B_GROUPS // 2, 2, B_CHUNK).transpose(0, 2, 1), B_GROUP_DIM, axis=2)
    kern = functools.partial(_even_core_kernel, tiles_per_seq=tps)
    return pl.pallas_call(
        kern,
        grid=(n // tq,),
        in_specs=[pl.BlockSpec((tq, A_Q), lambda i: (i, 0)),
                  pl.BlockSpec((tq, B_WIDTH), lambda i: (i, 1)),
                  pl.BlockSpec((tq, B_WIDTH), lambda i: (i, 2)),
                  pl.BlockSpec((tq, dup), lambda i: (i, kb)),
                  pl.BlockSpec((BLOCK, dup), lambda i: (prev(i), kb)),
                  pl.BlockSpec((tq, dup), lambda i: (i, kb + 1)),
                  pl.BlockSpec((BLOCK, dup), lambda i: (prev(i), kb + 1)),
                  pl.BlockSpec(memory_space=pltpu.SMEM),
                  pl.BlockSpec((A_KV_HEADS, A_GROUP * BLOCK, 2 * BLOCK), lambda i: (0, 0, 0)),
                  pl.BlockSpec((1, B_WIDTH), lambda i: (0, 0)),
                  pl.BlockSpec((1, B_WIDTH), lambda i: (0, 0)),
                  pl.BlockSpec((B_GROUPS, B_CHUNK, B_CHUNK), lambda i: (0, 0, 0)),
                  pl.BlockSpec((B_GROUPS // 2, B_CHUNK, LANES), lambda i: (0, 0, 0))],
        out_specs=[pl.BlockSpec((tq, A_Q), lambda i: (i, 0)),
                   pl.BlockSpec((tq, B_WIDTH), lambda i: (i, 0))],
        out_shape=[jax.ShapeDtypeStruct((n, A_Q), BF16), jax.ShapeDtypeStruct((n, B_WIDTH), BF16)],
        compiler_params=_params("arbitrary"),
        name="even_core",
    )(proj, proj, proj, proj, proj, proj, proj, sinks, _swa_bias(), ln_g.reshape(1, -1), ln_b.reshape(1, -1), w_s, bsp)


def _out_proj_kernel(m1_ref, m2_ref, h_ref, gate_ref, w_ref, o_ref):
    half = m1_ref.shape[1]
    y = _dot(m1_ref[...], w_ref[0:half, :]) + _dot(m2_ref[...], w_ref[half:2 * half, :])
    o_ref[...] = h_ref[...] + gate_ref[0] * y


def _out_proj(m1, m2, h, mod, w, seq, tm=512):
    n, d = h.shape
    tps = seq // tm
    half = m1.shape[1]
    return pl.pallas_call(
        _out_proj_kernel,
        grid=(n // tm,),
        in_specs=[pl.BlockSpec((tm, half), lambda i: (i, 0)),
                  pl.BlockSpec((tm, half), lambda i: (i, 0)),
                  pl.BlockSpec((tm, d), lambda i: (i, 0)),
                  _mod_spec(2, tps),
                  pl.BlockSpec((2 * half, d), lambda i: (0, 0))],
        out_specs=pl.BlockSpec((tm, d), lambda i: (i, 0)),
        out_shape=jax.ShapeDtypeStruct((n, d), F32),
        compiler_params=_params("arbitrary"),
        name="out_proj",
    )(m1, m2, h, mod, w)


def _keep_bf16_chunk(first, c, wg_ref, wu_ref, wd_ref, wgr, wur, wdr):
    @pl.when(first)
    def _():
        wgr[c] = wg_ref[...].reshape(wgr.shape[1:]).astype(BF16)
        wur[c] = wu_ref[...].reshape(wur.shape[1:]).astype(BF16)
        wdr[c] = wd_ref[...].reshape(wdr.shape[1:]).astype(BF16)


def _swiglu_chunk(x, c, wgr, wur, wdr):
    a = _dot(x, wgr[c])
    b = _dot(x, wur[c])
    return _dot((_silu(a) * b).astype(BF16), wdr[c])


def _ffn_kernel(h_ref, g_ref, sh_ref, sc_ref, gate_ref, wg_ref, wu_ref, wd_ref, o_ref, hn_ref, acc_ref,
                wgr, wur, wdr):
    c = pl.program_id(1)
    _keep_bf16_chunk(pl.program_id(0) == 0, c, wg_ref, wu_ref, wd_ref, wgr, wur, wdr)

    @pl.when(c == 0)
    def _():
        hn_ref[...] = _modulated_norm(h_ref[...], g_ref[...], sc_ref[0], sh_ref[0]).astype(BF16)
        acc_ref[...] = jnp.zeros_like(acc_ref)

    acc_ref[...] += _swiglu_chunk(hn_ref[...], c, wgr, wur, wdr)

    @pl.when(c == pl.num_programs(1) - 1)
    def _():
        o_ref[...] = h_ref[...] + gate_ref[0] * acc_ref[...]


def _ffn(h, mod, g, wg, wu, wd, seq, tm=1024, ch=256):
    n, d = h.shape
    ff = wg.shape[1]
    tps = seq // tm
    nc = ff // ch
    chunk = lambda i, c: jnp.where(i == 0, c, nc - 1)
    return pl.pallas_call(
        _ffn_kernel,
        grid=(n // tm, nc),
        in_specs=[pl.BlockSpec((tm, d), lambda i, c: (i, 0)),
                  pl.BlockSpec((1, d), lambda i, c: (0, 0)),
                  _mod_spec(3, tps), _mod_spec(4, tps), _mod_spec(5, tps),
                  pl.BlockSpec((d, ch), lambda i, c: (0, chunk(i, c))),
                  pl.BlockSpec((d, ch), lambda i, c: (0, chunk(i, c))),
                  pl.BlockSpec((ch, d), lambda i, c: (chunk(i, c), 0))],
        out_specs=pl.BlockSpec((tm, d), lambda i, c: (i, 0)),
        out_shape=jax.ShapeDtypeStruct((n, d), F32),
        scratch_shapes=[pltpu.VMEM((tm, d), BF16), pltpu.VMEM((tm, d), F32),
                        pltpu.VMEM((nc, d, ch), BF16), pltpu.VMEM((nc, d, ch), BF16), pltpu.VMEM((nc, ch, d), BF16)],
        compiler_params=_params("arbitrary", "arbitrary"),
        name="ffn",
    )(h, g.reshape(1, d), mod, mod, mod, wg, wu, wd)


def _odd_in_kernel(x_ref, g_ref, sh_ref, sc_ref, w_ref, cw_ref, qg_ref, wqa_ref, wqb_ref, kg_ref, wkk_ref, wkv_ref,
                   cos_ref, sin_ref, c_ref, q_ref, k_ref, v_ref, carry_ref, *, tiles_per_seq):
    first = (pl.program_id(0) % tiles_per_seq) == 0
    tm = x_ref.shape[0]
    hn = _modulated_norm(x_ref[...], g_ref[...], sc_ref[0], sh_ref[0]).astype(BF16)
    proj = _dot(hn, w_ref[...])
    o = 0
    gb = proj[:, o:o + C_WIDTH]; o += C_WIDTH
    gc = proj[:, o:o + C_WIDTH]; o += C_WIDTH
    xc = proj[:, o:o + C_WIDTH]; o += C_WIDTH
    q_lat = proj[:, o:o + D_Q_RANK]; o += D_Q_RANK
    kv_lat = proj[:, o:o + D_KV_RANK]; o += D_KV_RANK
    kr_a = proj[:, o:o + LANES]; o += LANES
    kr_b = proj[:, o:o + LANES]

    @pl.when(first)
    def _():
        carry_ref[...] = jnp.zeros_like(carry_ref)

    z = gc * xc
    prev1 = carry_ref[7:8, :]
    prev2 = carry_ref[6:7, :]
    row = lax.broadcasted_iota(jnp.int32, z.shape, 0)
    z1 = jnp.where(row == 0, prev1, pltpu.roll(z, 1, 0))
    z2 = jnp.where(row == 0, prev2, jnp.where(row == 1, prev1, pltpu.roll(z, 2, 0)))
    y = cw_ref[0:1, :] * z2 + cw_ref[1:2, :] * z1 + cw_ref[2:3, :] * z
    c_ref[...] = (gb * y).astype(c_ref.dtype)
    carry_ref[...] = z[tm - 8:tm, :]

    cosf = jnp.concatenate([cos_ref[...]] * D_HEADS, axis=1)
    sinf = jnp.concatenate([sin_ref[...]] * D_HEADS, axis=1)
    qn = (q_lat * lax.rsqrt(jnp.mean(q_lat * q_lat, axis=-1, keepdims=True) + NORM_EPS) * qg_ref[...]).astype(BF16)
    q = _dot(qn, wqa_ref[...]) * cosf + _dot(qn, wqb_ref[...]) * sinf
    q_ref[...] = (q * MLA_Q_SCALE).astype(q_ref.dtype)
    kn = (kv_lat * lax.rsqrt(jnp.mean(kv_lat * kv_lat, axis=-1, keepdims=True) + NORM_EPS) * kg_ref[...]).astype(BF16)
    kpe = kr_a * cos_ref[...] + kr_b * sin_ref[...]
    k_ref[...] = (_dot(kn, wkk_ref[...]) + jnp.concatenate([kpe] * D_HEADS, axis=1)).astype(k_ref.dtype)
    v_ref[...] = _dot(kn, wkv_ref[...]).astype(v_ref.dtype)


def _odd_in(h, mod, g, w, conv_w, qg, wqa, wqb, kg, wkk, wkv, cos_t, sin_t, seq, tm=512):
    n, d = h.shape
    tps = seq // tm
    hw = D_HEADS * D_HEAD_PAD
    const = lambda a: pl.BlockSpec(a.shape, lambda i: (0,) * a.ndim)
    qg = qg.reshape(1, -1)
    kg = kg.reshape(1, -1)
    kern = functools.partial(_odd_in_kernel, tiles_per_seq=tps)
    return pl.pallas_call(
        kern,
        grid=(n // tm,),
        in_specs=[pl.BlockSpec((tm, d), lambda i: (i, 0)),
                  pl.BlockSpec((1, d), lambda i: (0, 0)),
                  _mod_spec(0, tps), _mod_spec(1, tps),
                  const(w), const(conv_w), const(qg), const(wqa), const(wqb), const(kg), const(wkk), const(wkv),
                  pl.BlockSpec((tm, LANES), lambda i: (i % tps, 0)),
                  pl.BlockSpec((tm, LANES), lambda i: (i % tps, 0))],
        out_specs=[pl.BlockSpec((tm, C_WIDTH), lambda i: (i, 0)),
                   pl.BlockSpec((tm, hw), lambda i: (i, 0)),
                   pl.BlockSpec((tm, hw), lambda i: (i, 0)),
                   pl.BlockSpec((tm, D_HEADS * D_VDIM), lambda i: (i, 0))],
        out_shape=[jax.ShapeDtypeStruct((n, C_WIDTH), BF16),
                   jax.ShapeDtypeStruct((n, hw), BF16),
                   jax.ShapeDtypeStruct((n, hw), BF16),
                   jax.ShapeDtypeStruct((n, D_HEADS * D_VDIM), BF16)],
        scratch_shapes=[pltpu.VMEM((8, C_WIDTH), F32)],
        compiler_params=_params("arbitrary"),
        name="odd_in",
    )(h, g.reshape(1, d), mod, mod, w, conv_w, qg, wqa, wqb, kg, wkk, wkv, cos_t, sin_t)


MLA_T = 512


def _mla_kernel(qi_ref, kj_ref, q_ref, k_ref, v_ref, o_ref, m_ref, l_ref, acc_ref):
    t = pl.program_id(1)
    qi = qi_ref[t]
    kj = kj_ref[t]
    tq, tk = MLA_T, MLA_T
    rep = tk // LANES

    @pl.when(kj == 0)
    def _():
        m_ref[...] = jnp.full_like(m_ref, NEG_BIG)
        l_ref[...] = jnp.zeros_like(l_ref)
        acc_ref[...] = jnp.zeros_like(acc_ref)

    def tile(masked):
        if masked:
            row = lax.broadcasted_iota(jnp.int32, (tq, tk), 0)
            col = lax.broadcasted_iota(jnp.int32, (tq, tk), 1)
            keep = col <= row
        for hp in range(D_HEADS // 2):
            ps, alphas = [], []
            for h in (2 * hp, 2 * hp + 1):
                hl = slice(h * D_HEAD_PAD, (h + 1) * D_HEAD_PAD)
                s = _dot_nt(q_ref[:, hl], k_ref[:, hl])
                if masked:
                    s = jnp.where(keep, s, NEG_BIG)
                m_prev = m_ref[h]
                m_next = jnp.maximum(m_prev, jnp.max(s, axis=-1, keepdims=True))
                alpha = jnp.exp2(m_prev - m_next)
                p = jnp.exp2(s - jnp.concatenate([m_next] * rep, axis=1))
                l_ref[h] = alpha * l_ref[h] + jnp.sum(p, axis=-1, keepdims=True)
                m_ref[h] = m_next
                ps.append(p.astype(BF16))
                alphas.append(alpha)
            pv = _dot(jnp.concatenate(ps, axis=0), v_ref[:, hp * LANES:(hp + 1) * LANES])
            acc_ref[hp] = jnp.concatenate(alphas, axis=0) * acc_ref[hp] + pv

    @pl.when(kj < qi)
    def _():
        tile(False)

    @pl.when(kj == qi)
    def _():
        tile(True)
        lane = lax.broadcasted_iota(jnp.int32, (tq, LANES), 1)
        for hp in range(D_HEADS // 2):
            lo = acc_ref[hp, 0:tq, :] / l_ref[2 * hp]
            hi = acc_ref[hp, tq:2 * tq, :] / l_ref[2 * hp + 1]
            o_ref[:, hp * LANES:(hp + 1) * LANES] = jnp.where(lane < D_VDIM, lo, hi).astype(o_ref.dtype)


def _mla(qf, kf, v, batch, seq):
    n = qf.shape[0]
    t = MLA_T
    nq = seq // t
    pairs = [(i, j) for i in range(nq) for j in range(i + 1)]
    qi = jnp.asarray([p[0] for p in pairs], jnp.int32)
    kj = jnp.asarray([p[1] for p in pairs], jnp.int32)
    hw = D_HEADS * D_HEAD_PAD
    vw = D_HEADS * D_VDIM
    grid_spec = pltpu.PrefetchScalarGridSpec(
        num_scalar_prefetch=2,
        grid=(batch, len(pairs)),
        in_specs=[pl.BlockSpec((t, hw), lambda b, s, qi, kj: (b * nq + qi[s], 0)),
                  pl.BlockSpec((t, hw), lambda b, s, qi, kj: (b * nq + kj[s], 0)),
                  pl.BlockSpec((t, vw), lambda b, s, qi, kj: (b * nq + kj[s], 0))],
        out_specs=pl.BlockSpec((t, vw), lambda b, s, qi, kj: (b * nq + qi[s], 0)),
        scratch_shapes=[pltpu.VMEM((D_HEADS, t, LANES), F32),
                        pltpu.VMEM((D_HEADS, t, LANES), F32),
                        pltpu.VMEM((D_HEADS // 2, 2 * t, LANES), F32)],
    )
    return pl.pallas_call(
        _mla_kernel,
        grid_spec=grid_spec,
        out_shape=jax.ShapeDtypeStruct((n, vw), BF16),
        compiler_params=_params("arbitrary", "arbitrary"),
        name="mla",
    )(qi, kj, qf, kf, v)


ROUTER_TM = 512
ROW_TILE = D_MODEL // LANES


def _store_row_tiles(ref, x):
    rows = x.shape[0]
    for s in range(ROW_TILE):
        ref[pl.ds(s, rows, stride=ROW_TILE), :] = x[:, s * LANES:(s + 1) * LANES]


def _load_row_tiles(ref, rows):
    return jnp.concatenate([ref[pl.ds(s, rows, stride=ROW_TILE), :] for s in range(ROW_TILE)], axis=1)


def _tile_copy(src_ref, dst_ref, src_row, dst_row, sem):
    return pltpu.make_async_copy(src_ref.at[pl.ds(pl.multiple_of(src_row * ROW_TILE, ROW_TILE), ROW_TILE)],
                                 dst_ref.at[pl.ds(pl.multiple_of(dst_row * ROW_TILE, ROW_TILE), ROW_TILE)], sem)


def _router_kernel(h_ref, g_ref, sh_ref, sc_ref, wr_ref, br_ref, hn_ref, eid_ref, gate_ref, rank_ref, cnt_ref,
                   carry_ref):
    tm = h_ref.shape[0]

    @pl.when(pl.program_id(0) == 0)
    def _():
        carry_ref[...] = jnp.zeros_like(carry_ref)

    hn32 = _modulated_norm(h_ref[...], g_ref[...], sc_ref[0], sh_ref[0])
    _store_row_tiles(hn_ref, hn32)
    hn = hn32.astype(BF16)
    lane = lax.broadcasted_iota(jnp.int32, (tm, LANES), 1)
    logits = jnp.where(lane < N_EXPERTS, _dot(hn, wr_ref[...]) + br_ref[...], NEG_BIG)
    m1 = jnp.max(logits, axis=-1, keepdims=True)
    e1 = jnp.min(jnp.where(logits == m1, lane, LANES), axis=-1, keepdims=True)
    rest = jnp.where(lane == e1, NEG_BIG, logits)
    m2 = jnp.max(rest, axis=-1, keepdims=True)
    e2 = jnp.min(jnp.where(rest == m2, lane, LANES), axis=-1, keepdims=True)
    ex = jnp.exp(m2 - m1)
    g1 = 1.0 / (1.0 + ex)
    g2 = ex / (1.0 + ex)
    eid_ref[...] = jnp.where(lane == 0, e1, jnp.where(lane == 1, e2, 0))
    gate_ref[...] = jnp.where(lane == 0, g1, jnp.where(lane == 1, g2, 0.0))
    onehot = ((lane == e1) | (lane == e2)).astype(BF16)
    tt = lax.broadcasted_iota(jnp.int32, (tm, tm), 0)
    ss = lax.broadcasted_iota(jnp.int32, (tm, tm), 1)
    before = (ss < tt).astype(BF16)
    prefix = _dot(before, onehot) + carry_ref[...]
    r1 = jnp.sum(jnp.where(lane == e1, prefix, 0.0), axis=-1, keepdims=True)
    r2 = jnp.sum(jnp.where(lane == e2, prefix, 0.0), axis=-1, keepdims=True)
    rank_ref[...] = jnp.where(lane == 0, r1, jnp.where(lane == 1, r2, 0.0)).astype(jnp.int32)
    total = carry_ref[...] + jnp.sum(onehot.astype(F32), axis=0, keepdims=True)
    carry_ref[...] = total
    cnt_ref[...] = jnp.broadcast_to(total, cnt_ref.shape).astype(jnp.int32)


def _router(h, mod, g, wr, br, seq):
    n, d = h.shape
    tm = ROUTER_TM
    tps = seq // tm
    wide = lambda dt: jax.ShapeDtypeStruct((n, LANES), dt)
    row_spec = pl.BlockSpec((tm, LANES), lambda i: (i, 0))
    return pl.pallas_call(
        _router_kernel,
        grid=(n // tm,),
        in_specs=[pl.BlockSpec((tm, d), lambda i: (i, 0)),
                  pl.BlockSpec((1, d), lambda i: (0, 0)),
                  _mod_spec(3, tps), _mod_spec(4, tps),
                  pl.BlockSpec((d, LANES), lambda i: (0, 0)),
                  pl.BlockSpec((1, LANES), lambda i: (0, 0))],
        out_specs=[pl.BlockSpec((tm * ROW_TILE, LANES), lambda i: (i, 0)), row_spec, row_spec, row_spec,
                   pl.BlockSpec((8, LANES), lambda i: (0, 0))],
        out_shape=[jax.ShapeDtypeStruct((n * ROW_TILE, LANES), F32), wide(jnp.int32), wide(F32), wide(jnp.int32),
                   jax.ShapeDtypeStruct((8, LANES), jnp.int32)],
        scratch_shapes=[pltpu.VMEM((1, LANES), F32)],
        compiler_params=_params("arbitrary"),
        name="router",
    )(h, g.reshape(1, d), mod, mod, wr, br)


DISPATCH_TM = 512
ISSUE_UNROLL = 8


def _dispatch_kernel(d0_ref, d1_ref, zrow_ref, hn_ref, xs_ref, zero_ref, sem, zsem, *, tail_start):
    tm = DISPATCH_TM

    @pl.when(pl.program_id(0) == 0)
    def _():
        zero_ref[...] = jnp.zeros_like(zero_ref)
        blk = MOE_BLOCK * ROW_TILE
        tail = range(tail_start // MOE_BLOCK, xs_ref.shape[0] // blk)
        for start in [pl.multiple_of(zrow_ref[e] * ROW_TILE, ROW_TILE) for e in range(N_EXPERTS)] + \
                     [t * blk for t in tail]:
            clear = pltpu.make_async_copy(zero_ref, xs_ref.at[pl.ds(start, blk)], zsem)
            clear.start()
            clear.wait()

    def issue(r, carry):
        _tile_copy(hn_ref, xs_ref, r, d0_ref[0, 0, r], sem.at[0]).start(priority=0)
        _tile_copy(hn_ref, xs_ref, r, d1_ref[0, 0, r], sem.at[1]).start(priority=1)
        return carry

    lax.fori_loop(0, tm, issue, 0, unroll=ISSUE_UNROLL)
    pltpu.make_async_copy(hn_ref, xs_ref.at[pl.ds(0, tm * ROW_TILE)], sem.at[0]).wait()
    pltpu.make_async_copy(hn_ref, xs_ref.at[pl.ds(0, tm * ROW_TILE)], sem.at[1]).wait()


def _dispatch(hn_tiles, dest, zero_row, n_slots):
    tm = DISPATCH_TM
    n = dest.shape[0]
    nb = n // tm
    idx_spec = pl.BlockSpec((1, 1, tm), lambda i: (i, 0, 0), memory_space=pltpu.SMEM)
    return pl.pallas_call(
        functools.partial(_dispatch_kernel, tail_start=n * TOP_K),
        grid=(nb,),
        in_specs=[idx_spec, idx_spec,
                  pl.BlockSpec(memory_space=pltpu.SMEM),
                  pl.BlockSpec((tm * ROW_TILE, LANES), lambda i: (i, 0))],
        out_specs=pl.BlockSpec(memory_space=pl.ANY),
        out_shape=jax.ShapeDtypeStruct((n_slots * ROW_TILE, LANES), F32),
        scratch_shapes=[pltpu.VMEM((MOE_BLOCK * ROW_TILE, LANES), F32), pltpu.SemaphoreType.DMA((2,)),
                        pltpu.SemaphoreType.DMA(())],
        compiler_params=_params("arbitrary"),
        name="dispatch",
    )(dest[:, 0].reshape(nb, 1, tm), dest[:, 1].reshape(nb, 1, tm), zero_row, hn_tiles)


def _expert_kernel(be_ref, nu_ref, fi_ref, x_ref, wg_ref, wu_ref, wd_ref, o_ref, xb_ref, acc_ref, wgr, wur, wdr):
    b = pl.program_id(0)
    c = pl.program_id(1)
    used = b < nu_ref[0]
    bm = MOE_BLOCK
    _keep_bf16_chunk(fi_ref[b] == 1, c, wg_ref, wu_ref, wd_ref, wgr, wur, wdr)

    @pl.when(c == 0)
    def _():
        acc_ref[...] = jnp.zeros_like(acc_ref)
        xb_ref[...] = _load_row_tiles(x_ref, bm).astype(BF16)

    @pl.when(used)
    def _():
        acc_ref[...] += _swiglu_chunk(xb_ref[...], c, wgr, wur, wdr)

    @pl.when(c == pl.num_programs(1) - 1)
    def _():
        _store_row_tiles(o_ref, acc_ref[...])


def _experts(xs, blk_e, n_used, n_blk, wg, wu, wd, ch=512):
    d = D_MODEL
    ff = wg.shape[2]
    bm = MOE_BLOCK
    nc = ff // ch
    blocks = jnp.arange(n_blk, dtype=jnp.int32)
    first = ((blocks < n_used[0]) & ((blocks == 0) | (blk_e != jnp.roll(blk_e, 1)))).astype(jnp.int32)
    last = lambda b, nu: jnp.minimum(b, nu[0] - 1)
    chunk = lambda b, c, fi: jnp.where(fi[b] == 1, c, nc - 1)
    grid_spec = pltpu.PrefetchScalarGridSpec(
        num_scalar_prefetch=3,
        grid=(n_blk, nc),
        in_specs=[pl.BlockSpec((bm * ROW_TILE, LANES), lambda b, c, be, nu, fi: (last(b, nu), 0)),
                  pl.BlockSpec((1, d, ch), lambda b, c, be, nu, fi: (be[b], 0, chunk(b, c, fi))),
                  pl.BlockSpec((1, d, ch), lambda b, c, be, nu, fi: (be[b], 0, chunk(b, c, fi))),
                  pl.BlockSpec((1, ch, d), lambda b, c, be, nu, fi: (be[b], chunk(b, c, fi), 0))],
        out_specs=pl.BlockSpec((bm * ROW_TILE, LANES), lambda b, c, be, nu, fi: (b, 0)),
        scratch_shapes=[pltpu.VMEM((bm, d), BF16), pltpu.VMEM((bm, d), F32),
                        pltpu.VMEM((nc, d, ch), BF16), pltpu.VMEM((nc, d, ch), BF16), pltpu.VMEM((nc, ch, d), BF16)],
    )
    return pl.pallas_call(
        _expert_kernel,
        grid_spec=grid_spec,
        out_shape=jax.ShapeDtypeStruct((n_blk * bm * ROW_TILE, LANES), F32),
        compiler_params=_params("arbitrary", "arbitrary"),
        name="experts",
    )(blk_e, n_used, first, xs, wg, wu, wd)


COMBINE_TM = 256


def _combine_kernel(d0_ref, d1_ref, ys_ref, h_ref, gates_ref, gate_ref, fg_ref, o_ref, buf0, buf1, sem):
    tm = COMBINE_TM

    def issue(r, carry):
        _tile_copy(ys_ref, buf0, d0_ref[0, 0, r], r, sem.at[0]).start(priority=0)
        _tile_copy(ys_ref, buf1, d1_ref[0, 0, r], r, sem.at[1]).start(priority=1)
        return carry

    lax.fori_loop(0, tm, issue, 0, unroll=ISSUE_UNROLL)
    pltpu.make_async_copy(ys_ref.at[pl.ds(0, tm * ROW_TILE)], buf0, sem.at[0]).wait()
    pltpu.make_async_copy(ys_ref.at[pl.ds(0, tm * ROW_TILE)], buf1, sem.at[1]).wait()
    moe = gates_ref[:, 0:1] * _load_row_tiles(buf0, tm) + gates_ref[:, 1:2] * _load_row_tiles(buf1, tm)
    hh = h_ref[...] + gate_ref[0] * moe
    r = lax.rsqrt(jnp.mean(hh * hh, axis=-1, keepdims=True) + NORM_EPS)
    o_ref[...] = (hh * r) * fg_ref[...]


def _combine(ys, dest, gates, h, mod, final_g, seq):
    n, d = h.shape
    tm = COMBINE_TM
    tps = seq // tm
    nb = n // tm
    idx_spec = pl.BlockSpec((1, 1, tm), lambda i: (i, 0, 0), memory_space=pltpu.SMEM)
    return pl.pallas_call(
        _combine_kernel,
        grid=(nb,),
        in_specs=[idx_spec, idx_spec,
                  pl.BlockSpec(memory_space=pl.ANY),
                  pl.BlockSpec((tm, d), lambda i: (i, 0)),
                  pl.BlockSpec((tm, LANES), lambda i: (i, 0)),
                  _mod_spec(5, tps),
                  pl.BlockSpec((1, d), lambda i: (0, 0))],
        out_specs=pl.BlockSpec((tm, d), lambda i: (i, 0)),
        out_shape=jax.ShapeDtypeStruct((n, d), F32),
        scratch_shapes=[pltpu.VMEM((tm * ROW_TILE, LANES), F32), pltpu.VMEM((tm * ROW_TILE, LANES), F32),
                        pltpu.SemaphoreType.DMA((2,))],
        compiler_params=_params("arbitrary"),
        name="combine",
    )(dest[:, 0].reshape(nb, 1, tm), dest[:, 1].reshape(nb, 1, tm), ys, h, gates, mod, final_g.reshape(1, d))


def _moe(h, mod, g, wr, br, wg, wu, wd, final_g, seq):
    n, d = h.shape
    wr_p = jnp.zeros((d, LANES), BF16).at[:, :N_EXPERTS].set(wr.astype(BF16))
    br_p = jnp.zeros((1, LANES), F32).at[0, :N_EXPERTS].set(br)
    hn, eid, gates, rank, cnt = _router(h, mod, g, wr_p, br_p, seq)
    counts = cnt[0, :N_EXPERTS]
    padded = (counts + MOE_BLOCK - 1) // MOE_BLOCK * MOE_BLOCK
    pad_end = jnp.cumsum(padded)
    pad_start = pad_end - padded
    top_e = eid[:, :TOP_K]
    dest = pad_start[top_e] + rank[:, :TOP_K]
    na = n * TOP_K
    n_blk = -(-na // MOE_BLOCK) + N_EXPERTS
    blk_e = jnp.minimum(jnp.searchsorted(pad_end, jnp.arange(n_blk, dtype=jnp.int32) * MOE_BLOCK, side='right'),
                        N_EXPERTS - 1).astype(jnp.int32)
    n_used = (pad_end[-1:] // MOE_BLOCK).astype(jnp.int32)
    xs = _dispatch(hn, dest, (pad_start + counts).astype(jnp.int32), (n_blk + 1) * MOE_BLOCK)
    ys = _experts(xs, blk_e, n_used, n_blk, wg, wu, wd)
    return _combine(ys, dest, gates, h, mod, final_g, seq)


def _prep_even_w_in(w):
    q, k, v, u, g = jnp.split(w, [A_Q, A_Q + A_KV, A_Q + 2 * A_KV, A_Q + 2 * A_KV + B_WIDTH], axis=-1)
    twice = lambda a: jnp.concatenate([a[:, :A_HEAD_DIM], a[:, :A_HEAD_DIM], a[:, A_HEAD_DIM:], a[:, A_HEAD_DIM:]], -1)
    return jnp.concatenate([q, u, g, twice(k), twice(v)], axis=-1).astype(BF16)


def _prep_odd_w_in(w):
    d = w.shape[0]
    o = 3 * C_WIDTH + D_Q_RANK + D_KV_RANK
    half = D_ROPE // 2
    x1, x2 = w[:, o:o + half], w[:, o + half:o + D_ROPE]
    z64 = jnp.zeros((d, D_NOPE), w.dtype)
    z32 = jnp.zeros((d, D_HEAD_PAD - D_NOPE - D_ROPE), w.dtype)
    return jnp.concatenate([w[:, :o], z64, x1, x2, z32, z64, x2, x1, z32], axis=-1).astype(BF16)


def _prep_w_q_b(w):
    r = w.shape[0]
    half = D_ROPE // 2
    wh = w.reshape(r, D_HEADS, D_NOPE + D_ROPE)
    nope, x1, x2 = wh[..., :D_NOPE], wh[..., D_NOPE:D_NOPE + half], wh[..., D_NOPE + half:]
    z32 = jnp.zeros((r, D_HEADS, D_HEAD_PAD - D_NOPE - D_ROPE), w.dtype)
    wa = jnp.concatenate([nope, x1, x2, z32], axis=-1).reshape(r, D_HEADS * D_HEAD_PAD)
    wb = jnp.concatenate([jnp.zeros_like(nope), x2, x1, z32], axis=-1).reshape(r, D_HEADS * D_HEAD_PAD)
    return wa.astype(BF16), wb.astype(BF16)


def _prep_w_kv_b(w):
    r = w.shape[0]
    wh = w.reshape(r, D_HEADS, D_NOPE + D_VDIM)
    kk = jnp.concatenate([wh[..., :D_NOPE], jnp.zeros((r, D_HEADS, D_HEAD_PAD - D_NOPE), w.dtype)], axis=-1)
    return kk.reshape(r, D_HEADS * D_HEAD_PAD).astype(BF16), wh[..., D_NOPE:].reshape(r, D_HEADS * D_VDIM).astype(BF16)


def _rope_tables(seq):
    half = D_ROPE // 2
    inv = ROPE_THETA ** (-jnp.arange(0, D_ROPE, 2, dtype=F32) / D_ROPE)
    ang = jnp.arange(seq, dtype=F32)[:, None] * inv[None, :]
    cos, sin = jnp.cos(ang), jnp.sin(ang)
    one = jnp.ones((seq, D_NOPE), F32)
    z64 = jnp.zeros((seq, D_NOPE), F32)
    z32 = jnp.zeros((seq, D_HEAD_PAD - D_NOPE - D_ROPE), F32)
    assert half * 2 == D_ROPE
    return (jnp.concatenate([one, cos, cos, z32], axis=-1), jnp.concatenate([z64, -sin, sin, z32], axis=-1))


def kernel(x, c, even_ada_w, even_ada_b, even_norm_mix_g, even_w_in, even_sinks, even_gmlp_ln_g, even_gmlp_ln_b, even_w_s, even_b_s, even_w_o, even_norm_ffn_g, even_ffn_w_gate, even_ffn_w_up, even_ffn_w_down, odd_ada_w, odd_ada_b, odd_norm_mix_g, odd_w_in, odd_conv_w, odd_q_norm_g, odd_w_q_b, odd_kv_norm_g, odd_w_kv_b, odd_w_o, odd_norm_ffn_g, odd_router_w, odd_router_b, odd_exp_w_gate, odd_exp_w_up, odd_exp_w_down, final_norm_g):
    batch, seq, d = x.shape
    assert even_ada_w.shape[0] == 1 and odd_ada_w.shape[0] == 1, "one even and one odd layer"
    h = x.reshape(batch * seq, d)

    mod = _ada(c, even_ada_w[0], even_ada_b[0])
    proj = _even_in(h, mod, even_norm_mix_g[0], _prep_even_w_in(even_w_in[0]), seq)
    a_out, b_out = _even_core(proj, even_sinks[0], even_gmlp_ln_g[0], even_gmlp_ln_b[0], even_w_s[0], even_b_s[0], seq)
    h = _out_proj(a_out, b_out, h, mod, even_w_o[0].astype(BF16), seq)
    h = _ffn(h, mod, even_norm_ffn_g[0], even_ffn_w_gate[0], even_ffn_w_up[0], even_ffn_w_down[0], seq)

    mod = _ada(c, odd_ada_w[0], odd_ada_b[0])
    wqa, wqb = _prep_w_q_b(odd_w_q_b[0])
    wkk, wkv = _prep_w_kv_b(odd_w_kv_b[0])
    cos_t, sin_t = _rope_tables(seq)
    c_out, qf, kf, v = _odd_in(h, mod, odd_norm_mix_g[0], _prep_odd_w_in(odd_w_in[0]), odd_conv_w[0],
                               odd_q_norm_g[0], wqa, wqb, odd_kv_norm_g[0], wkk, wkv, cos_t, sin_t, seq)
    d_out = _mla(qf, kf, v, batch, seq)
    h = _out_proj(c_out, d_out, h, mod, odd_w_o[0].astype(BF16), seq)
    out = _moe(h, mod, odd_norm_ffn_g[0], odd_router_w[0], odd_router_b[0], odd_exp_w_gate[0], odd_exp_w_up[0],
               odd_exp_w_down[0], final_norm_g, seq)
    return out.reshape(batch, seq, d)
```

```python
import functools

import jax
import jax.numpy as jnp
import numpy as np
from jax import lax
from jax.experimental import pallas as pl
from jax.experimental.pallas import tpu as pltpu

F32 = jnp.float32
BF16 = jnp.bfloat16

D_MODEL = 1024
NORM_EPS = 1e-6
ADA_CHUNKS = 6
BLOCK = 128

A_HEADS = 8
A_KV_HEADS = 2
A_GROUP = A_HEADS // A_KV_HEADS
A_HEAD_DIM = 64
A_Q = A_HEADS * A_HEAD_DIM
A_KV = A_KV_HEADS * A_HEAD_DIM

B_GROUPS = 8
B_GROUP_DIM = 64
B_WIDTH = B_GROUPS * B_GROUP_DIM
B_CHUNK = 128

C_WIDTH = 512
C_CONV = 3

D_HEADS = 8
D_NOPE = 64
D_ROPE = 32
D_VDIM = 64
D_Q_RANK = 512
D_KV_RANK = 256
ROPE_THETA = 10000.0
D_HEAD_PAD = 128
LOG2E = 1.4426950408889634
MLA_Q_SCALE = np.float32((D_NOPE + D_ROPE) ** -0.5 * LOG2E)
SWA_Q_SCALE = np.float32(A_HEAD_DIM ** -0.5 * LOG2E)

N_EXPERTS = 8
TOP_K = 2
MOE_BLOCK = 512

LANES = 128
NEG_BIG = -1e30
VMEM_LIMIT = 56 * 1024 * 1024


def _params(*sem):
    return pltpu.CompilerParams(dimension_semantics=sem, vmem_limit_bytes=VMEM_LIMIT)


def _dot(a, b):
    return jnp.dot(a, b, preferred_element_type=F32)


def _dot_nt(a, b):
    return lax.dot_general(a, b, (((1,), (1,)), ((), ())), preferred_element_type=F32)


def _modulated_norm(x, g, scale, shift):
    ms = jnp.mean(x * x, axis=-1, keepdims=True)
    return (x * lax.rsqrt(ms + NORM_EPS)) * (g * (1.0 + scale)) + shift


def _gelu(x):
    return 0.5 * x * (1.0 + lax.erf(x * np.float32(0.7071067811865476)))


def _silu(x):
    return x * (1.0 / (1.0 + jnp.exp(-x)))


def _ada_kernel(c_ref, w_ref, b_ref, o_ref):
    cond = _silu(c_ref[...])
    o_ref[...] = _dot(cond.astype(BF16), w_ref[...].astype(BF16)) + b_ref[...]


def _ada(c, w, b):
    bn, d = c.shape
    rows = 8
    tn = 1536
    cp = jnp.zeros((rows, d), F32).at[:bn].set(c)
    out = pl.pallas_call(
        _ada_kernel,
        grid=(w.shape[1] // tn,),
        in_specs=[pl.BlockSpec((rows, d), lambda j: (0, 0)),
                  pl.BlockSpec((d, tn), lambda j: (0, j)),
                  pl.BlockSpec((1, tn), lambda j: (0, j))],
        out_specs=pl.BlockSpec((rows, tn), lambda j: (0, j)),
        out_shape=jax.ShapeDtypeStruct((rows, w.shape[1]), F32),
        compiler_params=_params("arbitrary"),
        name="ada",
    )(cp, w, b.reshape(1, -1))
    return out[:bn].reshape(bn * ADA_CHUNKS, 1, d)


def _mod_spec(chunk, tiles_per_seq):
    return pl.BlockSpec((1, 1, D_MODEL), lambda i, *_: ((i // tiles_per_seq) * ADA_CHUNKS + chunk, 0, 0))


def _even_in_kernel(x_ref, g_ref, sh_ref, sc_ref, w_ref, o_ref):
    hn = _modulated_norm(x_ref[...], g_ref[...], sc_ref[0], sh_ref[0])
    proj = _dot(hn.astype(BF16), w_ref[...])
    o_ref[:, 0:A_Q] = (proj[:, 0:A_Q] * SWA_Q_SCALE).astype(o_ref.dtype)
    o_ref[:, A_Q:] = proj[:, A_Q:].astype(o_ref.dtype)


def _even_in(h, mod, g, w, seq, tm=512):
    n, d = h.shape
    tps = seq // tm
    nout = w.shape[1]
    return pl.pallas_call(
        _even_in_kernel,
        grid=(n // tm,),
        in_specs=[pl.BlockSpec((tm, d), lambda i: (i, 0)),
                  pl.BlockSpec((1, d), lambda i: (0, 0)),
                  _mod_spec(0, tps), _mod_spec(1, tps),
                  pl.BlockSpec((d, nout), lambda i: (0, 0))],
        out_specs=pl.BlockSpec((tm, nout), lambda i: (i, 0)),
        out_shape=jax.ShapeDtypeStruct((n, nout), BF16),
        compiler_params=_params("arbitrary"),
        name="even_in",
    )(h, g.reshape(1, d), mod, mod, w)


EVEN_TQ = 512


def _even_core_kernel(q_ref, u_ref, gg_ref, k_ref, kp_ref, v_ref, vp_ref, sinks_ref, bias_ref, lng_ref, lnb_ref,
                      ws_ref, bsp_ref, a_ref, b_ref, *, tiles_per_seq):
    lane = lax.broadcasted_iota(jnp.int32, (BLOCK, LANES), 1)
    low = lane < A_HEAD_DIM
    first_key = jnp.where((pl.program_id(0) % tiles_per_seq) == 0, BLOCK, 0)
    kj = lax.broadcasted_iota(jnp.int32, (A_GROUP * BLOCK, 2 * BLOCK), 1)
    nsub = EVEN_TQ // BLOCK

    for sb in range(nsub):
        rows = slice(sb * BLOCK, (sb + 1) * BLOCK)
        prow = slice((sb - 1) * BLOCK, sb * BLOCK)
        for kvh in range(A_KV_HEADS):
            kl = slice(kvh * LANES, (kvh + 1) * LANES)
            kband = jnp.concatenate([kp_ref[:, kl] if sb == 0 else k_ref[prow, kl], k_ref[rows, kl]], axis=0)
            vband = jnp.concatenate([vp_ref[:, kl] if sb == 0 else v_ref[prow, kl], v_ref[rows, kl]], axis=0)
            stack = []
            for j in (2 * kvh, 2 * kvh + 1):
                qb = q_ref[rows, j * LANES:(j + 1) * LANES]
                stack += [jnp.where(low, qb, jnp.zeros_like(qb)), jnp.where(low, jnp.zeros_like(qb), qb)]
            s = _dot_nt(jnp.concatenate(stack, axis=0), kband) + bias_ref[kvh]
            if sb == 0:
                s = jnp.where(kj >= first_key, s, NEG_BIG)
            sink = jnp.concatenate(
                [jnp.full((BLOCK, 1), sinks_ref[kvh * A_GROUP + g] * np.float32(LOG2E)) for g in range(A_GROUP)],
                axis=0)
            m = jnp.maximum(jnp.max(s, axis=-1, keepdims=True), sink)
            p = jnp.exp2(s - m)
            denom = jnp.sum(p, axis=-1, keepdims=True) + jnp.exp2(sink - m)
            o = _dot(p.astype(BF16), vband) / denom
            for t, j in enumerate((2 * kvh, 2 * kvh + 1)):
                pair = jnp.where(low, o[2 * t * BLOCK:(2 * t + 1) * BLOCK], o[(2 * t + 1) * BLOCK:(2 * t + 2) * BLOCK])
                a_ref[rows, j * LANES:(j + 1) * LANES] = pair.astype(a_ref.dtype)

    gu = _gelu(u_ref[...].astype(F32))
    gv = _gelu(gg_ref[...].astype(F32))
    mu = jnp.mean(gv, axis=-1, keepdims=True)
    cen = gv - mu
    var = jnp.mean(cen * cen, axis=-1, keepdims=True)
    vn = ((cen * lax.rsqrt(var + NORM_EPS)) * lng_ref[...] + lnb_ref[...]).astype(BF16)
    tt = lax.broadcasted_iota(jnp.int32, (B_CHUNK, B_CHUNK), 0)
    ss = lax.broadcasted_iota(jnp.int32, (B_CHUNK, B_CHUNK), 1)
    causal = ss <= tt
    for jp in range(B_GROUPS // 2):
        bl = slice(jp * LANES, (jp + 1) * LANES)
        wcat = jnp.concatenate([jnp.where(causal, ws_ref[2 * jp], 0.0), jnp.where(causal, ws_ref[2 * jp + 1], 0.0)],
                               axis=1).astype(BF16)
        for c in range(EVEN_TQ // B_CHUNK):
            rows = slice(c * B_CHUNK, (c + 1) * B_CHUNK)
            vb = vn[rows, bl]
            rhs = jnp.concatenate([jnp.where(low, vb, jnp.zeros_like(vb)), jnp.where(low, jnp.zeros_like(vb), vb)],
                                  axis=0)
            b_ref[rows, bl] = (gu[rows, bl] * (_dot(wcat, rhs) + bsp_ref[jp])).astype(b_ref.dtype)


def _swa_bias():
    qi = np.arange(BLOCK)[:, None]
    kj = np.arange(2 * BLOCK)[None, :]
    dist = qi + BLOCK - kj
    valid = (dist >= 0) & (dist < BLOCK)
    slopes = 2.0 ** (-8.0 * np.arange(1, A_HEADS + 1) / A_HEADS)
    per_head = np.where(valid[None], -slopes[:, None, None] * dist[None] * LOG2E, NEG_BIG)
    return jnp.asarray(per_head.reshape(A_KV_HEADS, A_GROUP * BLOCK, 2 * BLOCK), F32)


def _even_core(proj, sinks, ln_g, ln_b, w_s, b_s, seq):
    n = proj.shape[0]
    tq = EVEN_TQ
    tps = seq // tq
    sub = tq // BLOCK
    dup = 2 * A_KV
    kb = (A_Q + 2 * B_WIDTH) // dup
    prev = lambda i: jnp.maximum(i * sub - 1, 0)
    bsp = jnp.repeat(b_s.reshape(B_GROUPS // 2, 2, B_CHUNK).transpose(0, 2, 1), B_GROUP_DIM, axis=2)
    kern = functools.partial(_even_core_kernel, tiles_per_seq=tps)
    return pl.pallas_call(
        kern,
        grid=(n // tq,),
        in_specs=[pl.BlockSpec((tq, A_Q), lambda i: (i, 0)),
                  pl.BlockSpec((tq, B_WIDTH), lambda i: (i, 1)),
                  pl.BlockSpec((tq, B_WIDTH), lambda i: (i, 2)),
                  pl.BlockSpec((tq, dup), lambda i: (i, kb)),
                  pl.BlockSpec((BLOCK, dup), lambda i: (prev(i), kb)),
                  pl.BlockSpec((tq, dup), lambda i: (i, kb + 1)),
                  pl.BlockSpec((BLOCK, dup), lambda i: (prev(i), kb + 1)),
                  pl.BlockSpec(memory_space=pltpu.SMEM),
                  pl.BlockSpec((A_KV_HEADS, A_GROUP * BLOCK, 2 * BLOCK), lambda i: (0, 0, 0)),
                  pl.BlockSpec((1, B_WIDTH), lambda i: (0, 0)),
                  pl.BlockSpec((1, B_WIDTH), lambda i: (0, 0)),
                  pl.BlockSpec((B_GROUPS, B_CHUNK, B_CHUNK), lambda i: (0, 0, 0)),
                  pl.BlockSpec((B_GROUPS // 2, B_CHUNK, LANES), lambda i: (0, 0, 0))],
        out_specs=[pl.BlockSpec((tq, A_Q), lambda i: (i, 0)),
                   pl.BlockSpec((tq, B_WIDTH), lambda i: (i, 0))],
        out_shape=[jax.ShapeDtypeStruct((n, A_Q), BF16), jax.ShapeDtypeStruct((n, B_WIDTH), BF16)],
        compiler_params=_params("arbitrary"),
        name="even_core",
    )(proj, proj, proj, proj, proj, proj, proj, sinks, _swa_bias(), ln_g.reshape(1, -1), ln_b.reshape(1, -1), w_s, bsp)


def _keep_bf16_chunk(first, c, wg_ref, wu_ref, wd_ref, wgr, wur, wdr):
    @pl.when(first)
    def _():
        wgr[c] = wg_ref[...].reshape(wgr.shape[1:]).astype(BF16)
        wur[c] = wu_ref[...].reshape(wur.shape[1:]).astype(BF16)
        wdr[c] = wd_ref[...].reshape(wdr.shape[1:]).astype(BF16)


def _swiglu_chunk(x, c, wgr, wur, wdr):
    a = _dot(x, wgr[c])
    b = _dot(x, wur[c])
    return _dot((_silu(a) * b).astype(BF16), wdr[c])


def _mixer_residual(m1_ref, m2_ref, h_ref, gate_ref, wo_ref):
    half = m1_ref.shape[1]
    y = _dot(m1_ref[...], wo_ref[0:half, :]) + _dot(m2_ref[...], wo_ref[half:2 * half, :])
    return h_ref[...] + gate_ref[0] * y


def _ffn_kernel(m1_ref, m2_ref, h_ref, g1_ref, wo_ref, g_ref, sh_ref, sc_ref, gate_ref, wg_ref, wu_ref, wd_ref, o_ref,
                hn_ref, acc_ref, wgr, wur, wdr):
    c = pl.program_id(1)
    _keep_bf16_chunk(pl.program_id(0) == 0, c, wg_ref, wu_ref, wd_ref, wgr, wur, wdr)

    @pl.when(c == 0)
    def _():
        h1 = _mixer_residual(m1_ref, m2_ref, h_ref, g1_ref, wo_ref)
        o_ref[...] = h1
        hn_ref[...] = _modulated_norm(h1, g_ref[...], sc_ref[0], sh_ref[0]).astype(BF16)
        acc_ref[...] = jnp.zeros_like(acc_ref)

    acc_ref[...] += _swiglu_chunk(hn_ref[...], c, wgr, wur, wdr)

    @pl.when(c == pl.num_programs(1) - 1)
    def _():
        o_ref[...] += gate_ref[0] * acc_ref[...]


def _ffn(m1, m2, h, mod, wo, g, wg, wu, wd, seq, tm=1024, ch=256):
    n, d = h.shape
    ff = wg.shape[1]
    half = m1.shape[1]
    tps = seq // tm
    nc = ff // ch
    chunk = lambda i, c: jnp.where(i == 0, c, nc - 1)
    return pl.pallas_call(
        _ffn_kernel,
        grid=(n // tm, nc),
        in_specs=[pl.BlockSpec((tm, half), lambda i, c: (i, 0)),
                  pl.BlockSpec((tm, half), lambda i, c: (i, 0)),
                  pl.BlockSpec((tm, d), lambda i, c: (i, 0)),
                  _mod_spec(2, tps),
                  pl.BlockSpec((2 * half, d), lambda i, c: (0, 0), pipeline_mode=pl.Buffered(1)),
                  pl.BlockSpec((1, d), lambda i, c: (0, 0)),
                  _mod_spec(3, tps), _mod_spec(4, tps), _mod_spec(5, tps),
                  pl.BlockSpec((d, ch), lambda i, c: (0, chunk(i, c))),
                  pl.BlockSpec((d, ch), lambda i, c: (0, chunk(i, c))),
                  pl.BlockSpec((ch, d), lambda i, c: (chunk(i, c), 0))],
        out_specs=pl.BlockSpec((tm, d), lambda i, c: (i, 0)),
        out_shape=jax.ShapeDtypeStruct((n, d), F32),
        scratch_shapes=[pltpu.VMEM((tm, d), BF16), pltpu.VMEM((tm, d), F32),
                        pltpu.VMEM((nc, d, ch), BF16), pltpu.VMEM((nc, d, ch), BF16), pltpu.VMEM((nc, ch, d), BF16)],
        compiler_params=_params("arbitrary", "arbitrary"),
        name="ffn",
    )(m1, m2, h, mod, wo, g.reshape(1, d), mod, mod, mod, wg, wu, wd)


def _odd_in_kernel(x_ref, g_ref, sh_ref, sc_ref, w_ref, cw_ref, qg_ref, wqa_ref, wqb_ref, kg_ref, wkk_ref, wkv_ref,
                   cos_ref, sin_ref, c_ref, q_ref, k_ref, v_ref, carry_ref, *, tiles_per_seq):
    first = (pl.program_id(0) % tiles_per_seq) == 0
    tm = x_ref.shape[0]
    hn = _modulated_norm(x_ref[...], g_ref[...], sc_ref[0], sh_ref[0]).astype(BF16)
    proj = _dot(hn, w_ref[...])
    o = 0
    gb = proj[:, o:o + C_WIDTH]; o += C_WIDTH
    gc = proj[:, o:o + C_WIDTH]; o += C_WIDTH
    xc = proj[:, o:o + C_WIDTH]; o += C_WIDTH
    q_lat = proj[:, o:o + D_Q_RANK]; o += D_Q_RANK
    kv_lat = proj[:, o:o + D_KV_RANK]; o += D_KV_RANK
    kr_a = proj[:, o:o + LANES]; o += LANES
    kr_b = proj[:, o:o + LANES]

    @pl.when(first)
    def _():
        carry_ref[...] = jnp.zeros_like(carry_ref)

    z = gc * xc
    prev1 = carry_ref[7:8, :]
    prev2 = carry_ref[6:7, :]
    row = lax.broadcasted_iota(jnp.int32, z.shape, 0)
    z1 = jnp.where(row == 0, prev1, pltpu.roll(z, 1, 0))
    z2 = jnp.where(row == 0, prev2, jnp.where(row == 1, prev1, pltpu.roll(z, 2, 0)))
    y = cw_ref[0:1, :] * z2 + cw_ref[1:2, :] * z1 + cw_ref[2:3, :] * z
    c_ref[...] = (gb * y).astype(c_ref.dtype)
    carry_ref[...] = z[tm - 8:tm, :]

    cosf = jnp.concatenate([cos_ref[...]] * D_HEADS, axis=1)
    sinf = jnp.concatenate([sin_ref[...]] * D_HEADS, axis=1)
    qn = (q_lat * lax.rsqrt(jnp.mean(q_lat * q_lat, axis=-1, keepdims=True) + NORM_EPS) * qg_ref[...]).astype(BF16)
    q = _dot(qn, wqa_ref[...]) * cosf + _dot(qn, wqb_ref[...]) * sinf
    q_ref[...] = (q * MLA_Q_SCALE).astype(q_ref.dtype)
    kn = (kv_lat * lax.rsqrt(jnp.mean(kv_lat * kv_lat, axis=-1, keepdims=True) + NORM_EPS) * kg_ref[...]).astype(BF16)
    kpe = kr_a * cos_ref[...] + kr_b * sin_ref[...]
    k_ref[...] = (_dot(kn, wkk_ref[...]) + jnp.concatenate([kpe] * D_HEADS, axis=1)).astype(k_ref.dtype)
    v_ref[...] = _dot(kn, wkv_ref[...]).astype(v_ref.dtype)


def _odd_in(h, mod, g, w, conv_w, qg, wqa, wqb, kg, wkk, wkv, cos_t, sin_t, seq, tm=512):
    n, d = h.shape
    tps = seq // tm
    hw = D_HEADS * D_HEAD_PAD
    const = lambda a: pl.BlockSpec(a.shape, lambda i: (0,) * a.ndim)
    qg = qg.reshape(1, -1)
    kg = kg.reshape(1, -1)
    kern = functools.partial(_odd_in_kernel, tiles_per_seq=tps)
    return pl.pallas_call(
        kern,
        grid=(n // tm,),
        in_specs=[pl.BlockSpec((tm, d), lambda i: (i, 0)),
                  pl.BlockSpec((1, d), lambda i: (0, 0)),
                  _mod_spec(0, tps), _mod_spec(1, tps),
                  const(w), const(conv_w), const(qg), const(wqa), const(wqb), const(kg), const(wkk), const(wkv),
                  pl.BlockSpec((tm, LANES), lambda i: (i % tps, 0)),
                  pl.BlockSpec((tm, LANES), lambda i: (i % tps, 0))],
        out_specs=[pl.BlockSpec((tm, C_WIDTH), lambda i: (i, 0)),
                   pl.BlockSpec((tm, hw), lambda i: (i, 0)),
                   pl.BlockSpec((tm, hw), lambda i: (i, 0)),
                   pl.BlockSpec((tm, D_HEADS * D_VDIM), lambda i: (i, 0))],
        out_shape=[jax.ShapeDtypeStruct((n, C_WIDTH), BF16),
                   jax.ShapeDtypeStruct((n, hw), BF16),
                   jax.ShapeDtypeStruct((n, hw), BF16),
                   jax.ShapeDtypeStruct((n, D_HEADS * D_VDIM), BF16)],
        scratch_shapes=[pltpu.VMEM((8, C_WIDTH), F32)],
        compiler_params=_params("arbitrary"),
        name="odd_in",
    )(h, g.reshape(1, d), mod, mod, w, conv_w, qg, wqa, wqb, kg, wkk, wkv, cos_t, sin_t)


MLA_T = 512


def _mla_kernel(qi_ref, kj_ref, q_ref, k_ref, v_ref, o_ref, m_ref, l_ref, acc_ref):
    t = pl.program_id(1)
    qi = qi_ref[t]
    kj = kj_ref[t]
    tq, tk = MLA_T, MLA_T
    rep = tk // LANES

    @pl.when(kj == 0)
    def _():
        m_ref[...] = jnp.full_like(m_ref, NEG_BIG)
        l_ref[...] = jnp.zeros_like(l_ref)
        acc_ref[...] = jnp.zeros_like(acc_ref)

    def tile(masked):
        if masked:
            row = lax.broadcasted_iota(jnp.int32, (tq, tk), 0)
            col = lax.broadcasted_iota(jnp.int32, (tq, tk), 1)
            keep = col <= row
        for hp in range(D_HEADS // 2):
            ps, alphas = [], []
            for h in (2 * hp, 2 * hp + 1):
                hl = slice(h * D_HEAD_PAD, (h + 1) * D_HEAD_PAD)
                s = _dot_nt(q_ref[:, hl], k_ref[:, hl])
                if masked:
                    s = jnp.where(keep, s, NEG_BIG)
                m_prev = m_ref[h]
                m_next = jnp.maximum(m_prev, jnp.max(s, axis=-1, keepdims=True))
                alpha = jnp.exp2(m_prev - m_next)
                p = jnp.exp2(s - jnp.concatenate([m_next] * rep, axis=1))
                l_ref[h] = alpha * l_ref[h] + jnp.sum(p, axis=-1, keepdims=True)
                m_ref[h] = m_next
                ps.append(p.astype(BF16))
                alphas.append(alpha)
            pv = _dot(jnp.concatenate(ps, axis=0), v_ref[:, hp * LANES:(hp + 1) * LANES])
            acc_ref[hp] = jnp.concatenate(alphas, axis=0) * acc_ref[hp] + pv

    @pl.when(kj < qi)
    def _():
        tile(False)

    @pl.when(kj == qi)
    def _():
        tile(True)
        lane = lax.broadcasted_iota(jnp.int32, (tq, LANES), 1)
        for hp in range(D_HEADS // 2):
            lo = acc_ref[hp, 0:tq, :] / l_ref[2 * hp]
            hi = acc_ref[hp, tq:2 * tq, :] / l_ref[2 * hp + 1]
            o_ref[:, hp * LANES:(hp + 1) * LANES] = jnp.where(lane < D_VDIM, lo, hi).astype(o_ref.dtype)


def _mla(qf, kf, v, batch, seq):
    n = qf.shape[0]
    t = MLA_T
    nq = seq // t
    pairs = [(i, j) for i in range(nq) for j in range(i + 1)]
    qi = jnp.asarray([p[0] for p in pairs], jnp.int32)
    kj = jnp.asarray([p[1] for p in pairs], jnp.int32)
    hw = D_HEADS * D_HEAD_PAD
    vw = D_HEADS * D_VDIM
    grid_spec = pltpu.PrefetchScalarGridSpec(
        num_scalar_prefetch=2,
        grid=(batch, len(pairs)),
        in_specs=[pl.BlockSpec((t, hw), lambda b, s, qi, kj: (b * nq + qi[s], 0)),
                  pl.BlockSpec((t, hw), lambda b, s, qi, kj: (b * nq + kj[s], 0)),
                  pl.BlockSpec((t, vw), lambda b, s, qi, kj: (b * nq + kj[s], 0))],
        out_specs=pl.BlockSpec((t, vw), lambda b, s, qi, kj: (b * nq + qi[s], 0)),
        scratch_shapes=[pltpu.VMEM((D_HEADS, t, LANES), F32),
                        pltpu.VMEM((D_HEADS, t, LANES), F32),
                        pltpu.VMEM((D_HEADS // 2, 2 * t, LANES), F32)],
    )
    return pl.pallas_call(
        _mla_kernel,
        grid_spec=grid_spec,
        out_shape=jax.ShapeDtypeStruct((n, vw), BF16),
        compiler_params=_params("arbitrary", "arbitrary"),
        name="mla",
    )(qi, kj, qf, kf, v)


ROUTER_TM = 512
ROW_TILE = D_MODEL // LANES


def _store_row_tiles(ref, x):
    rows = x.shape[0]
    for s in range(ROW_TILE):
        ref[pl.ds(s, rows, stride=ROW_TILE), :] = x[:, s * LANES:(s + 1) * LANES]


def _load_row_tiles(ref, rows):
    return jnp.concatenate([ref[pl.ds(s, rows, stride=ROW_TILE), :] for s in range(ROW_TILE)], axis=1)


def _tile_copy(src_ref, dst_ref, src_row, dst_row, sem):
    return pltpu.make_async_copy(src_ref.at[pl.ds(pl.multiple_of(src_row * ROW_TILE, ROW_TILE), ROW_TILE)],
                                 dst_ref.at[pl.ds(pl.multiple_of(dst_row * ROW_TILE, ROW_TILE), ROW_TILE)], sem)


def _router_kernel(m1_ref, m2_ref, h_ref, g1_ref, wo_ref, g_ref, sh_ref, sc_ref, wr_ref, br_ref,
                   h1_ref, hn_ref, eid_ref, gate_ref, rank_ref, cnt_ref, carry_ref):
    tm = h_ref.shape[0]

    @pl.when(pl.program_id(0) == 0)
    def _():
        carry_ref[...] = jnp.zeros_like(carry_ref)

    h1 = _mixer_residual(m1_ref, m2_ref, h_ref, g1_ref, wo_ref)
    h1_ref[...] = h1
    hn32 = _modulated_norm(h1, g_ref[...], sc_ref[0], sh_ref[0])
    _store_row_tiles(hn_ref, hn32)
    hn = hn32.astype(BF16)
    lane = lax.broadcasted_iota(jnp.int32, (tm, LANES), 1)
    logits = jnp.where(lane < N_EXPERTS, _dot(hn, wr_ref[...]) + br_ref[...], NEG_BIG)
    m1 = jnp.max(logits, axis=-1, keepdims=True)
    e1 = jnp.min(jnp.where(logits == m1, lane, LANES), axis=-1, keepdims=True)
    rest = jnp.where(lane == e1, NEG_BIG, logits)
    m2 = jnp.max(rest, axis=-1, keepdims=True)
    e2 = jnp.min(jnp.where(rest == m2, lane, LANES), axis=-1, keepdims=True)
    ex = jnp.exp(m2 - m1)
    g1 = 1.0 / (1.0 + ex)
    g2 = ex / (1.0 + ex)
    eid_ref[...] = jnp.where(lane == 0, e1, jnp.where(lane == 1, e2, 0))
    gate_ref[...] = jnp.where(lane == 0, g1, jnp.where(lane == 1, g2, 0.0))
    onehot = ((lane == e1) | (lane == e2)).astype(BF16)
    tt = lax.broadcasted_iota(jnp.int32, (tm, tm), 0)
    ss = lax.broadcasted_iota(jnp.int32, (tm, tm), 1)
    before = (ss < tt).astype(BF16)
    prefix = _dot(before, onehot) + carry_ref[...]
    r1 = jnp.sum(jnp.where(lane == e1, prefix, 0.0), axis=-1, keepdims=True)
    r2 = jnp.sum(jnp.where(lane == e2, prefix, 0.0), axis=-1, keepdims=True)
    rank_ref[...] = jnp.where(lane == 0, r1, jnp.where(lane == 1, r2, 0.0)).astype(jnp.int32)
    total = carry_ref[...] + jnp.sum(onehot.astype(F32), axis=0, keepdims=True)
    carry_ref[...] = total
    cnt_ref[...] = jnp.broadcast_to(total, cnt_ref.shape).astype(jnp.int32)


def _router(m1, m2, h, mod, wo, g, wr, br, seq):
    n, d = h.shape
    tm = ROUTER_TM
    tps = seq // tm
    half = m1.shape[1]
    wide = lambda dt: jax.ShapeDtypeStruct((n, LANES), dt)
    row_spec = pl.BlockSpec((tm, LANES), lambda i: (i, 0))
    return pl.pallas_call(
        _router_kernel,
        grid=(n // tm,),
        in_specs=[pl.BlockSpec((tm, half), lambda i: (i, 0)),
                  pl.BlockSpec((tm, half), lambda i: (i, 0)),
                  pl.BlockSpec((tm, d), lambda i: (i, 0)),
                  _mod_spec(2, tps),
                  pl.BlockSpec((2 * half, d), lambda i: (0, 0)),
                  pl.BlockSpec((1, d), lambda i: (0, 0)),
                  _mod_spec(3, tps), _mod_spec(4, tps),
                  pl.BlockSpec((d, LANES), lambda i: (0, 0)),
                  pl.BlockSpec((1, LANES), lambda i: (0, 0))],
        out_specs=[pl.BlockSpec((tm, d), lambda i: (i, 0)),
                   pl.BlockSpec((tm * ROW_TILE, LANES), lambda i: (i, 0)), row_spec, row_spec, row_spec,
                   pl.BlockSpec((8, LANES), lambda i: (0, 0))],
        out_shape=[jax.ShapeDtypeStruct((n, d), F32),
                   jax.ShapeDtypeStruct((n * ROW_TILE, LANES), F32), wide(jnp.int32), wide(F32), wide(jnp.int32),
                   jax.ShapeDtypeStruct((8, LANES), jnp.int32)],
        scratch_shapes=[pltpu.VMEM((1, LANES), F32)],
        compiler_params=_params("arbitrary"),
        name="router",
    )(m1, m2, h, mod, wo, g.reshape(1, d), mod, mod, wr, br)


DISPATCH_TM = 512
ISSUE_UNROLL = 8


def _dispatch_kernel(d0_ref, d1_ref, zrow_ref, hn_ref, xs_ref, zero_ref, sem, zsem, *, tail_start):
    tm = DISPATCH_TM

    @pl.when(pl.program_id(0) == 0)
    def _():
        zero_ref[...] = jnp.zeros_like(zero_ref)
        blk = MOE_BLOCK * ROW_TILE
        tail = range(tail_start // MOE_BLOCK, xs_ref.shape[0] // blk)
        for start in [pl.multiple_of(zrow_ref[e] * ROW_TILE, ROW_TILE) for e in range(N_EXPERTS)] + \
                     [t * blk for t in tail]:
            clear = pltpu.make_async_copy(zero_ref, xs_ref.at[pl.ds(start, blk)], zsem)
            clear.start()
            clear.wait()

    def issue(r, carry):
        _tile_copy(hn_ref, xs_ref, r, d0_ref[0, 0, r], sem.at[0]).start(priority=0)
        _tile_copy(hn_ref, xs_ref, r, d1_ref[0, 0, r], sem.at[1]).start(priority=1)
        return carry

    lax.fori_loop(0, tm, issue, 0, unroll=ISSUE_UNROLL)
    pltpu.make_async_copy(hn_ref, xs_ref.at[pl.ds(0, tm * ROW_TILE)], sem.at[0]).wait()
    pltpu.make_async_copy(hn_ref, xs_ref.at[pl.ds(0, tm * ROW_TILE)], sem.at[1]).wait()


def _dispatch(hn_tiles, dest, zero_row, n_slots):
    tm = DISPATCH_TM
    n = dest.shape[0]
    nb = n // tm
    idx_spec = pl.BlockSpec((1, 1, tm), lambda i: (i, 0, 0), memory_space=pltpu.SMEM)
    return pl.pallas_call(
        functools.partial(_dispatch_kernel, tail_start=n * TOP_K),
        grid=(nb,),
        in_specs=[idx_spec, idx_spec,
                  pl.BlockSpec(memory_space=pltpu.SMEM),
                  pl.BlockSpec((tm * ROW_TILE, LANES), lambda i: (i, 0))],
        out_specs=pl.BlockSpec(memory_space=pl.ANY),
        out_shape=jax.ShapeDtypeStruct((n_slots * ROW_TILE, LANES), F32),
        scratch_shapes=[pltpu.VMEM((MOE_BLOCK * ROW_TILE, LANES), F32), pltpu.SemaphoreType.DMA((2,)),
                        pltpu.SemaphoreType.DMA(())],
        compiler_params=_params("arbitrary"),
        name="dispatch",
    )(dest[:, 0].reshape(nb, 1, tm), dest[:, 1].reshape(nb, 1, tm), zero_row, hn_tiles)


def _expert_kernel(be_ref, nu_ref, fi_ref, x_ref, wg_ref, wu_ref, wd_ref, o_ref, xb_ref, acc_ref, wgr, wur, wdr):
    b = pl.program_id(0)
    c = pl.program_id(1)
    used = b < nu_ref[0]
    bm = MOE_BLOCK
    _keep_bf16_chunk(fi_ref[b] == 1, c, wg_ref, wu_ref, wd_ref, wgr, wur, wdr)

    @pl.when(c == 0)
    def _():
        acc_ref[...] = jnp.zeros_like(acc_ref)
        xb_ref[...] = _load_row_tiles(x_ref, bm).astype(BF16)

    @pl.when(used)
    def _():
        acc_ref[...] += _swiglu_chunk(xb_ref[...], c, wgr, wur, wdr)

    @pl.when(c == pl.num_programs(1) - 1)
    def _():
        _store_row_tiles(o_ref, acc_ref[...])


def _experts(xs, blk_e, n_used, n_blk, wg, wu, wd, ch=512):
    d = D_MODEL
    ff = wg.shape[2]
    bm = MOE_BLOCK
    nc = ff // ch
    blocks = jnp.arange(n_blk, dtype=jnp.int32)
    first = ((blocks < n_used[0]) & ((blocks == 0) | (blk_e != jnp.roll(blk_e, 1)))).astype(jnp.int32)
    last = lambda b, nu: jnp.minimum(b, nu[0] - 1)
    chunk = lambda b, c, fi: jnp.where(fi[b] == 1, c, nc - 1)
    grid_spec = pltpu.PrefetchScalarGridSpec(
        num_scalar_prefetch=3,
        grid=(n_blk, nc),
        in_specs=[pl.BlockSpec((bm * ROW_TILE, LANES), lambda b, c, be, nu, fi: (last(b, nu), 0)),
                  pl.BlockSpec((1, d, ch), lambda b, c, be, nu, fi: (be[b], 0, chunk(b, c, fi))),
                  pl.BlockSpec((1, d, ch), lambda b, c, be, nu, fi: (be[b], 0, chunk(b, c, fi))),
                  pl.BlockSpec((1, ch, d), lambda b, c, be, nu, fi: (be[b], chunk(b, c, fi), 0))],
        out_specs=pl.BlockSpec((bm * ROW_TILE, LANES), lambda b, c, be, nu, fi: (b, 0)),
        scratch_shapes=[pltpu.VMEM((bm, d), BF16), pltpu.VMEM((bm, d), F32),
                        pltpu.VMEM((nc, d, ch), BF16), pltpu.VMEM((nc, d, ch), BF16), pltpu.VMEM((nc, ch, d), BF16)],
    )
    return pl.pallas_call(
        _expert_kernel,
        grid_spec=grid_spec,
        out_shape=jax.ShapeDtypeStruct((n_blk * bm * ROW_TILE, LANES), F32),
        compiler_params=_params("arbitrary", "arbitrary"),
        name="experts",
    )(blk_e, n_used, first, xs, wg, wu, wd)


COMBINE_TM = 256


def _combine_kernel(d0_ref, d1_ref, d0n_ref, d1n_ref, ys_ref, h_ref, gates_ref, gate_ref, fg_ref, o_ref,
                    buf0, buf1, sem):
    tm = COMBINE_TM
    i = pl.program_id(0)
    slot = i % 2

    def fetch(i0_ref, i1_ref, s):
        def issue(r, carry):
            _tile_copy(ys_ref, buf0.at[s], i0_ref[0, 0, r], r, sem.at[s, 0]).start(priority=0)
            _tile_copy(ys_ref, buf1.at[s], i1_ref[0, 0, r], r, sem.at[s, 1]).start(priority=1)
            return carry

        lax.fori_loop(0, tm, issue, 0, unroll=ISSUE_UNROLL)

    @pl.when(i == 0)
    def _():
        fetch(d0_ref, d1_ref, 0)

    @pl.when(i + 1 < pl.num_programs(0))
    def _():
        fetch(d0n_ref, d1n_ref, 1 - slot)

    pltpu.make_async_copy(ys_ref.at[pl.ds(0, tm * ROW_TILE)], buf0.at[slot], sem.at[slot, 0]).wait()
    pltpu.make_async_copy(ys_ref.at[pl.ds(0, tm * ROW_TILE)], buf1.at[slot], sem.at[slot, 1]).wait()
    moe = (gates_ref[:, 0:1] * _load_row_tiles(buf0.at[slot], tm)
           + gates_ref[:, 1:2] * _load_row_tiles(buf1.at[slot], tm))
    hh = h_ref[...] + gate_ref[0] * moe
    r = lax.rsqrt(jnp.mean(hh * hh, axis=-1, keepdims=True) + NORM_EPS)
    o_ref[...] = (hh * r) * fg_ref[...]


def _combine(ys, dest, gates, h, mod, final_g, seq):
    n, d = h.shape
    tm = COMBINE_TM
    tps = seq // tm
    nb = n // tm
    idx_spec = pl.BlockSpec((1, 1, tm), lambda i: (i, 0, 0), memory_space=pltpu.SMEM)
    nxt_spec = pl.BlockSpec((1, 1, tm), lambda i: (jnp.minimum(i + 1, nb - 1), 0, 0), memory_space=pltpu.SMEM)
    d0 = dest[:, 0].reshape(nb, 1, tm)
    d1 = dest[:, 1].reshape(nb, 1, tm)
    return pl.pallas_call(
        _combine_kernel,
        grid=(nb,),
        in_specs=[idx_spec, idx_spec, nxt_spec, nxt_spec,
                  pl.BlockSpec(memory_space=pl.ANY),
                  pl.BlockSpec((tm, d), lambda i: (i, 0)),
                  pl.BlockSpec((tm, LANES), lambda i: (i, 0)),
                  _mod_spec(5, tps),
                  pl.BlockSpec((1, d), lambda i: (0, 0))],
        out_specs=pl.BlockSpec((tm, d), lambda i: (i, 0)),
        out_shape=jax.ShapeDtypeStruct((n, d), F32),
        scratch_shapes=[pltpu.VMEM((2, tm * ROW_TILE, LANES), F32), pltpu.VMEM((2, tm * ROW_TILE, LANES), F32),
                        pltpu.SemaphoreType.DMA((2, 2))],
        compiler_params=_params("arbitrary"),
        name="combine",
    )(d0, d1, d0, d1, ys, h, gates, mod, final_g.reshape(1, d))


def _moe(m1, m2, h, mod, wo, g, wr, br, wg, wu, wd, final_g, seq):
    n, d = h.shape
    wr_p = jnp.zeros((d, LANES), BF16).at[:, :N_EXPERTS].set(wr.astype(BF16))
    br_p = jnp.zeros((1, LANES), F32).at[0, :N_EXPERTS].set(br)
    h, hn, eid, gates, rank, cnt = _router(m1, m2, h, mod, wo, g, wr_p, br_p, seq)
    counts = cnt[0, :N_EXPERTS]
    padded = (counts + MOE_BLOCK - 1) // MOE_BLOCK * MOE_BLOCK
    pad_end = jnp.cumsum(padded)
    pad_start = pad_end - padded
    top_e = eid[:, :TOP_K]
    dest = pad_start[top_e] + rank[:, :TOP_K]
    na = n * TOP_K
    n_blk = -(-na // MOE_BLOCK) + N_EXPERTS
    blk_row = jnp.arange(n_blk, dtype=jnp.int32) * MOE_BLOCK
    blk_e = jnp.minimum(jnp.sum(pad_end[None, :] <= blk_row[:, None], axis=1), N_EXPERTS - 1).astype(jnp.int32)
    n_used = (pad_end[-1:] // MOE_BLOCK).astype(jnp.int32)
    xs = _dispatch(hn, dest, (pad_start + counts).astype(jnp.int32), (n_blk + 1) * MOE_BLOCK)
    ys = _experts(xs, blk_e, n_used, n_blk, wg, wu, wd)
    return _combine(ys, dest, gates, h, mod, final_g, seq)


def _prep_even_w_in(w):
    q, k, v, u, g = jnp.split(w, [A_Q, A_Q + A_KV, A_Q + 2 * A_KV, A_Q + 2 * A_KV + B_WIDTH], axis=-1)
    twice = lambda a: jnp.concatenate([a[:, :A_HEAD_DIM], a[:, :A_HEAD_DIM], a[:, A_HEAD_DIM:], a[:, A_HEAD_DIM:]], -1)
    return jnp.concatenate([q, u, g, twice(k), twice(v)], axis=-1).astype(BF16)


def _prep_odd_w_in(w):
    d = w.shape[0]
    o = 3 * C_WIDTH + D_Q_RANK + D_KV_RANK
    half = D_ROPE // 2
    x1, x2 = w[:, o:o + half], w[:, o + half:o + D_ROPE]
    z64 = jnp.zeros((d, D_NOPE), w.dtype)
    z32 = jnp.zeros((d, D_HEAD_PAD - D_NOPE - D_ROPE), w.dtype)
    return jnp.concatenate([w[:, :o], z64, x1, x2, z32, z64, x2, x1, z32], axis=-1).astype(BF16)


def _prep_w_q_b(w):
    r = w.shape[0]
    half = D_ROPE // 2
    wh = w.reshape(r, D_HEADS, D_NOPE + D_ROPE)
    nope, x1, x2 = wh[..., :D_NOPE], wh[..., D_NOPE:D_NOPE + half], wh[..., D_NOPE + half:]
    z32 = jnp.zeros((r, D_HEADS, D_HEAD_PAD - D_NOPE - D_ROPE), w.dtype)
    wa = jnp.concatenate([nope, x1, x2, z32], axis=-1).reshape(r, D_HEADS * D_HEAD_PAD)
    wb = jnp.concatenate([jnp.zeros_like(nope), x2, x1, z32], axis=-1).reshape(r, D_HEADS * D_HEAD_PAD)
    return wa.astype(BF16), wb.astype(BF16)


def _prep_w_kv_b(w):
    r = w.shape[0]
    wh = w.reshape(r, D_HEADS, D_NOPE + D_VDIM)
    kk = jnp.concatenate([wh[..., :D_NOPE], jnp.zeros((r, D_HEADS, D_HEAD_PAD - D_NOPE), w.dtype)], axis=-1)
    return kk.reshape(r, D_HEADS * D_HEAD_PAD).astype(BF16), wh[..., D_NOPE:].reshape(r, D_HEADS * D_VDIM).astype(BF16)


def _rope_tables(seq):
    half = D_ROPE // 2
    inv = ROPE_THETA ** (-jnp.arange(0, D_ROPE, 2, dtype=F32) / D_ROPE)
    ang = jnp.arange(seq, dtype=F32)[:, None] * inv[None, :]
    cos, sin = jnp.cos(ang), jnp.sin(ang)
    one = jnp.ones((seq, D_NOPE), F32)
    z64 = jnp.zeros((seq, D_NOPE), F32)
    z32 = jnp.zeros((seq, D_HEAD_PAD - D_NOPE - D_ROPE), F32)
    assert half * 2 == D_ROPE
    return (jnp.concatenate([one, cos, cos, z32], axis=-1), jnp.concatenate([z64, -sin, sin, z32], axis=-1))


def kernel(x, c, even_ada_w, even_ada_b, even_norm_mix_g, even_w_in, even_sinks, even_gmlp_ln_g, even_gmlp_ln_b, even_w_s, even_b_s, even_w_o, even_norm_ffn_g, even_ffn_w_gate, even_ffn_w_up, even_ffn_w_down, odd_ada_w, odd_ada_b, odd_norm_mix_g, odd_w_in, odd_conv_w, odd_q_norm_g, odd_w_q_b, odd_kv_norm_g, odd_w_kv_b, odd_w_o, odd_norm_ffn_g, odd_router_w, odd_router_b, odd_exp_w_gate, odd_exp_w_up, odd_exp_w_down, final_norm_g):
    batch, seq, d = x.shape
    assert even_ada_w.shape[0] == 1 and odd_ada_w.shape[0] == 1, "one even and one odd layer"
    h = x.reshape(batch * seq, d)

    mod = _ada(c, even_ada_w[0], even_ada_b[0])
    proj = _even_in(h, mod, even_norm_mix_g[0], _prep_even_w_in(even_w_in[0]), seq)
    a_out, b_out = _even_core(proj, even_sinks[0], even_gmlp_ln_g[0], even_gmlp_ln_b[0], even_w_s[0], even_b_s[0], seq)
    h = _ffn(a_out, b_out, h, mod, even_w_o[0].astype(BF16), even_norm_ffn_g[0], even_ffn_w_gate[0],
             even_ffn_w_up[0], even_ffn_w_down[0], seq)

    mod = _ada(c, odd_ada_w[0], odd_ada_b[0])
    wqa, wqb = _prep_w_q_b(odd_w_q_b[0])
    wkk, wkv = _prep_w_kv_b(odd_w_kv_b[0])
    cos_t, sin_t = _rope_tables(seq)
    c_out, qf, kf, v = _odd_in(h, mod, odd_norm_mix_g[0], _prep_odd_w_in(odd_w_in[0]), odd_conv_w[0],
                               odd_q_norm_g[0], wqa, wqb, odd_kv_norm_g[0], wkk, wkv, cos_t, sin_t, seq)
    d_out = _mla(qf, kf, v, batch, seq)
    out = _moe(c_out, d_out, h, mod, odd_w_o[0].astype(BF16), odd_norm_ffn_g[0], odd_router_w[0], odd_router_b[0],
               odd_exp_w_gate[0], odd_exp_w_up[0], odd_exp_w_down[0], final_norm_g, seq)
    return out.reshape(batch, seq, d)
```
